```python
import math
import jax, jax.numpy as jnp
from jax import lax
import numpy as np

D_MODEL = 1024
BATCH = 8
SEQ = 2048
DEPTH = 4

ATTN_HEADS = 8
HEAD_DIM = 64
ATTN_WIDTH = ATTN_HEADS * HEAD_DIM
CONV_GROUPS = 8
CONV_WIDTH = D_MODEL - ATTN_WIDTH
IN_COLS = 3 * ATTN_WIDTH + 2 * CONV_WIDTH
MOBA_BLOCK = 256
MOBA_TOPK = 3
Q_CHUNK = 64
ROPE_THETA = 500000.0
ROT_DIM = HEAD_DIM // 4
CONV_KERNEL = 31
N_EXPERTS = 32
TOP_K = 4
D_FF = D_MODEL
SWIGLU_LIMIT = 7.0
SWIGLU_ALPHA = 1.702
MOE_BLOCK = 256
DEEPNORM_ALPHA = (2.0 * DEPTH) ** 0.25
DEEPNORM_BETA = (8.0 * DEPTH) ** -0.25
LN_EPS = 1e-5

kernel_name = "hybrid_moba_conformer_moe_deepnorm"


def layer_norm(x, g, b):
    xf = x.astype(jnp.float32)
    mu = jnp.mean(xf, axis=-1, keepdims=True)
    var = jnp.mean(jnp.square(xf - mu), axis=-1, keepdims=True)
    return ((xf - mu) * lax.rsqrt(var + LN_EPS) * g + b).astype(x.dtype)


def rms_norm(x, g):
    xf = x.astype(jnp.float32)
    return (xf * lax.rsqrt(jnp.mean(jnp.square(xf), axis=-1, keepdims=True) + LN_EPS) * g).astype(x.dtype)


def rotary_tables(positions):
    inv_freq = ROPE_THETA ** (-jnp.arange(0, ROT_DIM, 2, dtype=jnp.float32) / ROT_DIM)
    ang = positions.astype(jnp.float32)[..., None] * inv_freq
    return jnp.cos(ang)[:, :, None, :], jnp.sin(ang)[:, :, None, :]


def partial_rotary(x, cos, sin):
    xr = x[..., :ROT_DIM].astype(jnp.float32)
    x1, x2 = xr[..., :ROT_DIM // 2], xr[..., ROT_DIM // 2:]
    rot = jnp.concatenate([x1 * cos - x2 * sin, x2 * cos + x1 * sin], axis=-1)
    return jnp.concatenate([rot.astype(x.dtype), x[..., ROT_DIM:]], axis=-1)


def moba_attention(q, k, v):
    B, H, S, Dh = q.shape
    nb = -(-S // MOBA_BLOCK)
    pad = nb * MOBA_BLOCK - S
    kp = jnp.pad(k, ((0, 0), (0, 0), (0, pad), (0, 0)))
    vp = jnp.pad(v, ((0, 0), (0, 0), (0, pad), (0, 0)))
    k_blocks = kp.reshape(B, H, nb, MOBA_BLOCK, Dh)
    v_blocks = vp.reshape(B, H, nb, MOBA_BLOCK, Dh)
    n_valid = jnp.clip(S - jnp.arange(nb) * MOBA_BLOCK, 1, MOBA_BLOCK).astype(jnp.float32)
    k_mean = (jnp.sum(k_blocks.astype(jnp.float32), axis=3) / n_valid[:, None]).astype(k.dtype)
    topk = min(MOBA_TOPK, nb)
    scale = HEAD_DIM ** -0.5
    n_chunks = S // Q_CHUNK
    q_chunks = q.reshape(B, H, n_chunks, Q_CHUNK, Dh).transpose(2, 0, 1, 3, 4)
    bi = jnp.arange(B)[:, None, None, None]
    hi = jnp.arange(H)[None, :, None, None]
    blk_ids = jnp.arange(nb)

    def chunk_fn(args):
        c, qc = args
        q_start = c * Q_CHUNK
        own = q_start // MOBA_BLOCK
        gate = jnp.einsum('bhqd,bhnd->bhqn', qc, k_mean)
        gate = jnp.where(blk_ids < own, gate, -jnp.inf)
        _, gidx = lax.top_k(gate, topk)
        gvalid = gidx < own
        k_sel = k_blocks[bi, hi, gidx]
        v_sel = v_blocks[bi, hi, gidx]
        s_sel = jnp.einsum('bhqd,bhqnkd->bhqnk', qc, k_sel).astype(jnp.float32) * scale
        s_sel = jnp.where(gvalid[..., None], s_sel, -jnp.inf).reshape(B, H, Q_CHUNK, topk * MOBA_BLOCK)
        k_own = lax.dynamic_index_in_dim(k_blocks, own, axis=2, keepdims=False)
        v_own = lax.dynamic_index_in_dim(v_blocks, own, axis=2, keepdims=False)
        s_own = jnp.einsum('bhqd,bhkd->bhqk', qc, k_own).astype(jnp.float32) * scale
        q_pos = q_start + jnp.arange(Q_CHUNK)
        k_pos = own * MOBA_BLOCK + jnp.arange(MOBA_BLOCK)
        s_own = jnp.where(k_pos[None, :] <= q_pos[:, None], s_own, -jnp.inf)
        p = jax.nn.softmax(jnp.concatenate([s_sel, s_own], axis=-1), axis=-1).astype(v.dtype)
        p_sel = p[..., :topk * MOBA_BLOCK].reshape(B, H, Q_CHUNK, topk, MOBA_BLOCK)
        p_own = p[..., topk * MOBA_BLOCK:]
        return (jnp.einsum('bhqnk,bhqnkd->bhqd', p_sel, v_sel)
                + jnp.einsum('bhqk,bhkd->bhqd', p_own, v_own))

    out = lax.map(chunk_fn, (jnp.arange(n_chunks), q_chunks))
    return out.transpose(1, 0, 3, 2, 4).reshape(B, S, H * Dh)


def conformer_conv(u2, conv_w, conv_b, ln_g, ln_b):
    a, g = jnp.split(u2, 2, axis=-1)
    u = a * jax.nn.sigmoid(g)
    u = lax.conv_general_dilated(
        u, conv_w[:, None, :].astype(u.dtype), window_strides=(1,),
        padding=[(CONV_KERNEL - 1, 0)], dimension_numbers=('NWC', 'WIO', 'NWC'),
        feature_group_count=CONV_WIDTH) + conv_b
    return jax.nn.silu(layer_norm(u, ln_g, ln_b))


def moe_ffn(x2d, w_router, b_router, w_gu, b_gu, w_dn, b_dn):
    T, D = x2d.shape
    TK = T * TOP_K
    logits = (x2d @ w_router + b_router).astype(jnp.float32)
    top_val, top_idx = lax.top_k(logits, TOP_K)
    gates = jax.nn.softmax(top_val, axis=-1).astype(x2d.dtype)
    flat_e = top_idx.reshape(-1)
    flat_tok = jnp.arange(TK, dtype=jnp.int32) // TOP_K
    order = jnp.argsort(flat_e)
    sorted_e = flat_e[order]
    sorted_tok = flat_tok[order]
    counts = jnp.bincount(flat_e, length=N_EXPERTS)
    padded = ((counts + MOE_BLOCK - 1) // MOE_BLOCK) * MOE_BLOCK
    pad_end = jnp.cumsum(padded)
    pad_start = pad_end - padded
    grp_start = jnp.cumsum(counts) - counts
    dest = pad_start[sorted_e] + (jnp.arange(TK) - grp_start[sorted_e])
    n_blocks = TK // MOE_BLOCK + N_EXPERTS
    buf_tok = jnp.full((n_blocks * MOE_BLOCK,), T, jnp.int32).at[dest].set(sorted_tok)
    block_e = jnp.clip(jnp.searchsorted(pad_end, jnp.arange(n_blocks) * MOE_BLOCK, side='right'),
                       0, N_EXPERTS - 1)
    x_pad = jnp.concatenate([x2d, jnp.zeros((1, D), x2d.dtype)], axis=0)

    def block_fn(args):
        tok, e = args
        xb = x_pad[tok]
        h = xb @ w_gu[e] + b_gu[e]
        x_glu, x_lin = jnp.split(h, 2, axis=-1)
        x_glu = jnp.minimum(x_glu, SWIGLU_LIMIT)
        x_lin = jnp.clip(x_lin, -SWIGLU_LIMIT, SWIGLU_LIMIT)
        act = x_glu * jax.nn.sigmoid(SWIGLU_ALPHA * x_glu) * (x_lin + 1.0)
        return act @ w_dn[e] + b_dn[e]

    y_buf = lax.map(block_fn, (buf_tok.reshape(n_blocks, MOE_BLOCK), block_e))
    y_assign = y_buf.reshape(-1, D)[dest]
    g_sorted = gates.reshape(-1)[order]
    return jax.ops.segment_sum(y_assign * g_sorted[:, None], sorted_tok, num_segments=T)


def setup_inputs(seed: int = 0) -> dict:
    key = jax.random.key(seed)
    ks = jax.random.split(key, 20)
    L, D = DEPTH, D_MODEL
    f32 = jnp.float32

    def nrm(k, shape, scale):
        return jax.random.normal(k, shape, f32) * scale

    x = jax.random.normal(ks[0], (BATCH, SEQ, D), f32)
    positions = jnp.broadcast_to(jnp.arange(SEQ, dtype=jnp.int32)[None, :], (BATCH, SEQ))
    w_in = nrm(ks[1], (L, D, IN_COLS), D ** -0.5)
    w_in = w_in.at[:, :, 2 * ATTN_WIDTH:3 * ATTN_WIDTH].multiply(DEEPNORM_BETA)
    attn_gain = 1.0 + nrm(ks[2], (L, ATTN_WIDTH), 0.01)
    conv_w = nrm(ks[3], (L, CONV_KERNEL, CONV_WIDTH), CONV_KERNEL ** -0.5)
    conv_b = nrm(ks[4], (L, CONV_WIDTH), 0.01)
    conv_ln_g = 1.0 + nrm(ks[5], (L, CONV_WIDTH), 0.01)
    conv_ln_b = nrm(ks[6], (L, CONV_WIDTH), 0.01)
    w_o = nrm(ks[7], (L, D, D), D ** -0.5 * DEEPNORM_BETA)
    ln1_g = 1.0 + nrm(ks[8], (L, D), 0.01)
    ln1_b = nrm(ks[9], (L, D), 0.01)
    w_router = nrm(ks[10], (L, D, N_EXPERTS), D ** -0.5)
    b_router = nrm(ks[11], (L, N_EXPERTS), 0.01)
    w_gu = nrm(ks[12], (L, N_EXPERTS, D, 2 * D_FF), D ** -0.5)
    b_gu = nrm(ks[13], (L, N_EXPERTS, 2 * D_FF), 0.01)
    w_dn = nrm(ks[14], (L, N_EXPERTS, D_FF, D), D_FF ** -0.5 * DEEPNORM_BETA)
    b_dn = nrm(ks[15], (L, N_EXPERTS, D), 0.01)
    ln2_g = 1.0 + nrm(ks[16], (L, D), 0.01)
    ln2_b = nrm(ks[17], (L, D), 0.01)
    return {"x": x, "positions": positions, "w_in": w_in, "attn_gain": attn_gain,
            "conv_w": conv_w, "conv_b": conv_b, "conv_ln_g": conv_ln_g, "conv_ln_b": conv_ln_b,
            "w_o": w_o, "ln1_g": ln1_g, "ln1_b": ln1_b, "w_router": w_router, "b_router": b_router,
            "w_gu": w_gu, "b_gu": b_gu, "w_dn": w_dn, "b_dn": b_dn, "ln2_g": ln2_g, "ln2_b": ln2_b}


def reference(x, positions, w_in, attn_gain, conv_w, conv_b, conv_ln_g, conv_ln_b,
              w_o, ln1_g, ln1_b, w_router, b_router, w_gu, b_gu, w_dn, b_dn, ln2_g, ln2_b):
    B, S, D = x.shape
    cos, sin = rotary_tables(positions)
    for l in range(DEPTH):
        h = x @ w_in[l]
        q = h[..., :ATTN_WIDTH].reshape(B, S, ATTN_HEADS, HEAD_DIM)
        k = h[..., ATTN_WIDTH:2 * ATTN_WIDTH].reshape(B, S, ATTN_HEADS, HEAD_DIM)
        v = h[..., 2 * ATTN_WIDTH:3 * ATTN_WIDTH].reshape(B, S, ATTN_HEADS, HEAD_DIM)
        u2 = h[..., 3 * ATTN_WIDTH:]
        q = partial_rotary(q, cos, sin).transpose(0, 2, 1, 3)
        k = partial_rotary(k, cos, sin).transpose(0, 2, 1, 3)
        v = v.transpose(0, 2, 1, 3)
        y_attn = rms_norm(moba_attention(q, k, v), attn_gain[l])
        y_conv = conformer_conv(u2, conv_w[l], conv_b[l], conv_ln_g[l], conv_ln_b[l])
        mix = jnp.concatenate([y_attn, y_conv], axis=-1) @ w_o[l]
        x = layer_norm(DEEPNORM_ALPHA * x + mix, ln1_g[l], ln1_b[l])
        y = moe_ffn(x.reshape(B * S, D), w_router[l], b_router[l], w_gu[l], b_gu[l],
                    w_dn[l], b_dn[l]).reshape(B, S, D)
        x = layer_norm(DEEPNORM_ALPHA * x + y, ln2_g[l], ln2_b[l])
    return x
```

```python
import functools

import jax
import jax.numpy as jnp
from jax import lax
from jax.experimental import pallas as pl
from jax.experimental.pallas import tpu as pltpu

F32 = jnp.float32
BF16 = jnp.bfloat16
I32 = jnp.int32

HEAD_DIM = 64
ATTN_HEADS = 8
ATTN_WIDTH = ATTN_HEADS * HEAD_DIM
CONV_WIDTH = 512
ROT_DIM = HEAD_DIM // 4
ROPE_THETA = 500000.0
MOBA_BLOCK = 256
MOBA_TOPK = 3
CONV_KERNEL = 31
N_EXPERTS = 32
TOP_K = 4
MOE_BLOCK = 256
SWIGLU_LIMIT = 7.0
SWIGLU_ALPHA = 1.702
DEEPNORM_ALPHA = (2.0 * 4) ** 0.25
LN_EPS = 1e-5

LANES = 128
CONV_HALO = 32
MASK_PENALTY = -1e30
VMEM_LIMIT = 48 * 1024 * 1024

_NT = (((1,), (1,)), ((), ()))


def _params(*sem):
    return pltpu.CompilerParams(dimension_semantics=sem, vmem_limit_bytes=VMEM_LIMIT)


def _dot(a, b):
    return jnp.dot(a, b, preferred_element_type=F32)


def _dot_nt(a, b):
    return lax.dot_general(a, b, _NT, preferred_element_type=F32)


def _split_bf16(a):
    hi = a.astype(BF16)
    lo = (a - hi.astype(F32)).astype(BF16)
    return hi, lo


def _layer_norm(z, g, b):
    mu = jnp.mean(z, axis=-1, keepdims=True)
    zc = z - mu
    var = jnp.mean(zc * zc, axis=-1, keepdims=True)
    return zc * lax.rsqrt(var + LN_EPS) * g + b


def _qkv_kernel(x_ref, w_ref, c_ref, s1_ref, s2_ref, o_ref):
    xb = x_ref[...].astype(BF16)
    reps = ATTN_WIDTH // LANES
    c = jnp.concatenate([c_ref[...]] * reps, axis=1)
    s1 = jnp.concatenate([s1_ref[...]] * reps, axis=1)
    s2 = jnp.concatenate([s2_ref[...]] * reps, axis=1)
    half = ROT_DIM // 2
    for part in range(3):
        cols = slice(part * ATTN_WIDTH, (part + 1) * ATTN_WIDTH)
        h = _dot(xb, w_ref[:, cols])
        if part < 2:
            h = h * c + pltpu.roll(h, ATTN_WIDTH - half, 1) * s1 + pltpu.roll(h, half, 1) * s2
        o_ref[:, cols] = h.astype(BF16)


def _qkv_proj(x2d, w_qkv, rot_c, rot_s1, rot_s2, tm):
    T, D = x2d.shape
    N = 3 * ATTN_WIDTH
    return pl.pallas_call(
        _qkv_kernel,
        grid=(T // tm,),
        in_specs=[
            pl.BlockSpec((tm, D), lambda i: (i, 0)),
            pl.BlockSpec((D, N), lambda i: (0, 0)),
            pl.BlockSpec((tm, LANES), lambda i: (i, 0)),
            pl.BlockSpec((tm, LANES), lambda i: (i, 0)),
            pl.BlockSpec((tm, LANES), lambda i: (i, 0)),
        ],
        out_specs=pl.BlockSpec((tm, N), lambda i: (i, 0)),
        out_shape=jax.ShapeDtypeStruct((T, N), BF16),
        compiler_params=_params("parallel"),
        name="qkv_proj",
    )(x2d, w_qkv, rot_c, rot_s1, rot_s2)


def _moba_kernel(q_ref, k_ref, v_ref, o_ref, kaug_ref, kmean_ref):
    qb = pl.program_id(2)
    S = k_ref.shape[1]
    nb = S // MOBA_BLOCK
    BQ = MOBA_BLOCK

    @pl.when(qb == 0)
    def _prepare_keys():
        kf = k_ref[0].astype(F32)
        kmean_ref[...] = jnp.sum(kf.reshape(nb, MOBA_BLOCK, LANES), axis=1) / float(MOBA_BLOCK)
        lane = lax.broadcasted_iota(I32, (S, LANES), 1)
        blk = lax.broadcasted_iota(I32, (S, LANES), 0) // MOBA_BLOCK
        for h in range(2):
            own = (lane >= h * HEAD_DIM) & (lane < (h + 1) * HEAD_DIM)
            spare = (1 - h) * HEAD_DIM
            ind = jnp.where(lane == spare + blk, 1.0, 0.0)
            kaug_ref[h] = jnp.where(own, kf, ind).astype(BF16)

    qf = q_ref[0].astype(F32)
    q16 = q_ref[0]
    lane = lax.broadcasted_iota(I32, (BQ, LANES), 1)
    lane8 = lax.broadcasted_iota(I32, (nb, LANES), 1)
    jidx = lax.broadcasted_iota(I32, (nb, BQ), 0)
    row = lax.broadcasted_iota(I32, (BQ, BQ), 0)
    col = lax.broadcasted_iota(I32, (BQ, BQ), 1)
    causal = col <= row
    scale = HEAD_DIM ** -0.5
    outs = []
    for h in range(2):
        own = (lane >= h * HEAD_DIM) & (lane < (h + 1) * HEAD_DIM)
        own8 = (lane8 >= h * HEAD_DIM) & (lane8 < (h + 1) * HEAD_DIM)
        spare = (1 - h) * HEAD_DIM
        km_hi, km_lo = _split_bf16(jnp.where(own8, kmean_ref[...], 0.0))
        gate = _dot_nt(km_hi, q16) + _dot_nt(km_lo, q16)
        valid = jidx < qb
        gm = jnp.where(valid, gate, -jnp.inf)
        rank = jnp.zeros((nb, BQ), F32)
        for j2 in range(nb):
            other = gm[j2:j2 + 1, :]
            beats = (other > gm) | ((other == gm) & (j2 < jidx))
            rank = rank + jnp.where(beats, 1.0, 0.0)
        keep = (rank < float(MOBA_TOPK)) | (jidx >= qb)
        pen_t = jnp.where(keep, 0.0, MASK_PENALTY)
        pieces = []
        if spare > 0:
            pieces.append(jnp.zeros((spare, BQ), F32))
        pieces.append(pen_t)
        pieces.append(jnp.zeros((LANES - spare - nb, BQ), F32))
        pen = jnp.concatenate(pieces, axis=0).T
        q_aug = (jnp.where(own, qf * scale, 0.0) + pen).astype(BF16)

        r0 = pl.multiple_of(qb * MOBA_BLOCK, MOBA_BLOCK)
        s = _dot_nt(q_aug, kaug_ref[h, pl.ds(r0, MOBA_BLOCK), :])
        s = jnp.where(causal, s, -jnp.inf)
        m0 = jnp.max(s, axis=-1, keepdims=True)
        p = jnp.exp(s - m0)
        l0 = jnp.sum(p, axis=-1, keepdims=True)
        acc0 = _dot(p.astype(BF16), v_ref[0, pl.ds(r0, MOBA_BLOCK), :])

        def past(j, carry):
            m, l, acc = carry
            rj = pl.multiple_of(j * MOBA_BLOCK, MOBA_BLOCK)
            sj = _dot_nt(q_aug, kaug_ref[h, pl.ds(rj, MOBA_BLOCK), :])
            m_new = jnp.maximum(m, jnp.max(sj, axis=-1, keepdims=True))
            a = jnp.exp(m - m_new)
            pj = jnp.exp(sj - m_new)
            l = a * l + jnp.sum(pj, axis=-1, keepdims=True)
            acc = a * acc + _dot(pj.astype(BF16), v_ref[0, pl.ds(rj, MOBA_BLOCK), :])
            return m_new, l, acc

        _, l, acc = lax.fori_loop(0, qb, past, (m0, l0, acc0))
        outs.append(acc / l)
    o_ref[0] = jnp.where(lane < HEAD_DIM, outs[0], outs[1])


def _moba_attention(qkv3):
    B, S, _ = qkv3.shape
    n_pairs = ATTN_WIDTH // LANES
    nb = S // MOBA_BLOCK
    return pl.pallas_call(
        _moba_kernel,
        grid=(B, n_pairs, nb),
        in_specs=[
            pl.BlockSpec((1, MOBA_BLOCK, LANES), lambda b, p, i: (b, i, p)),
            pl.BlockSpec((1, S, LANES), lambda b, p, i: (b, 0, n_pairs + p)),
            pl.BlockSpec((1, S, LANES), lambda b, p, i: (b, 0, 2 * n_pairs + p)),
        ],
        out_specs=pl.BlockSpec((1, MOBA_BLOCK, LANES), lambda b, p, i: (b, i, p)),
        out_shape=jax.ShapeDtypeStruct((B, S, ATTN_WIDTH), F32),
        scratch_shapes=[pltpu.VMEM((2, S, LANES), BF16), pltpu.VMEM((nb, LANES), F32)],
        compiler_params=_params("parallel", "parallel", "arbitrary"),
        name="moba_attention",
    )(qkv3, qkv3, qkv3)


def _conv_kernel(x_ref, w_ref, cw_ref, cb_ref, g_ref, b_ref, o_ref, ubuf):
    si = pl.program_id(1)
    tm = x_ref.shape[1]

    @pl.when(si == 0)
    def _zero_history():
        ubuf[0:CONV_HALO, :] = jnp.zeros((CONV_HALO, CONV_WIDTH), F32)

    @pl.when(si > 0)
    def _carry_history():
        ubuf[0:CONV_HALO, :] = ubuf[tm:tm + CONV_HALO, :]

    u2 = _dot(x_ref[0].astype(BF16), w_ref[...])
    ubuf[CONV_HALO:CONV_HALO + tm, :] = u2[:, :CONV_WIDTH] * jax.nn.sigmoid(u2[:, CONV_WIDTH:])
    acc = jnp.broadcast_to(cb_ref[...], (tm, CONV_WIDTH))
    base = CONV_HALO - (CONV_KERNEL - 1)
    for j in range(CONV_KERNEL):
        acc = acc + cw_ref[j:j + 1, :] * ubuf[base + j:base + j + tm, :]
    y = _layer_norm(acc, g_ref[...], b_ref[...])
    o_ref[0] = (y * jax.nn.sigmoid(y)).astype(BF16)


def _conv_branch(x3, w_cv, cw, cb, g, b, tm):
    B, S, D = x3.shape
    return pl.pallas_call(
        _conv_kernel,
        grid=(B, S // tm),
        in_specs=[
            pl.BlockSpec((1, tm, D), lambda bb, i: (bb, i, 0)),
            pl.BlockSpec((D, 2 * CONV_WIDTH), lambda bb, i: (0, 0)),
            pl.BlockSpec((CONV_KERNEL, CONV_WIDTH), lambda bb, i: (0, 0)),
            pl.BlockSpec((1, CONV_WIDTH), lambda bb, i: (0, 0)),
            pl.BlockSpec((1, CONV_WIDTH), lambda bb, i: (0, 0)),
            pl.BlockSpec((1, CONV_WIDTH), lambda bb, i: (0, 0)),
        ],
        out_specs=pl.BlockSpec((1, tm, CONV_WIDTH), lambda bb, i: (bb, i, 0)),
        out_shape=jax.ShapeDtypeStruct((B, S, CONV_WIDTH), BF16),
        scratch_shapes=[pltpu.VMEM((CONV_HALO + tm, CONV_WIDTH), F32)],
        compiler_params=_params("parallel", "arbitrary"),
        name="conv_branch",
    )(x3, w_cv, cw, cb, g, b)


def _mix_kernel(ya_ref, yc_ref, x_ref, gain_ref, wo_ref, g_ref, b_ref, wrh_ref, wrl_ref, br_ref,
                x1_ref, idx_ref, gate_ref):
    tm = x_ref.shape[0]
    ya = ya_ref[...]
    ms = jnp.mean(ya * ya, axis=-1, keepdims=True)
    yan = ya * lax.rsqrt(ms + LN_EPS) * gain_ref[...]
    mix = _dot(yan.astype(BF16), wo_ref[0:ATTN_WIDTH, :]) + _dot(yc_ref[...], wo_ref[ATTN_WIDTH:, :])
    x1 = _layer_norm(DEEPNORM_ALPHA * x_ref[...] + mix, g_ref[...], b_ref[...])
    x1_ref[...] = x1
    xh, xl = _split_bf16(x1)
    wrh = wrh_ref[...]
    logit = _dot_nt(wrh, xh) + _dot_nt(wrl_ref[...], xh) + _dot_nt(wrh, xl) + br_ref[...]
    eidx = lax.broadcasted_iota(I32, (N_EXPERTS, tm), 0)
    vals, idxs = [], []
    for _ in range(TOP_K):
        m = jnp.max(logit, axis=0, keepdims=True)
        idx = jnp.min(jnp.where(logit == m, eidx, N_EXPERTS), axis=0, keepdims=True)
        vals.append(m)
        idxs.append(idx)
        logit = jnp.where(eidx == idx, -jnp.inf, logit)
    es = [jnp.exp(v - vals[0]) for v in vals]
    tot = es[0] + es[1] + es[2] + es[3]
    idx_ref[...] = jnp.concatenate(idxs, axis=0)
    gate_ref[...] = jnp.concatenate([e / tot for e in es], axis=0)


def _mix_router(ya, yc, x2d, gain, w_o, g, b, wr_hi, wr_lo, b_r, tm):
    T, D = x2d.shape
    full = lambda shape: pl.BlockSpec(shape, lambda i: (0, 0))
    return pl.pallas_call(
        _mix_kernel,
        grid=(T // tm,),
        in_specs=[
            pl.BlockSpec((tm, ATTN_WIDTH), lambda i: (i, 0)),
            pl.BlockSpec((tm, CONV_WIDTH), lambda i: (i, 0)),
            pl.BlockSpec((tm, D), lambda i: (i, 0)),
            full((1, ATTN_WIDTH)),
            full((D, D)),
            full((1, D)),
            full((1, D)),
            full((N_EXPERTS, D)),
            full((N_EXPERTS, D)),
            full((N_EXPERTS, 1)),
        ],
        out_specs=[
            pl.BlockSpec((tm, D), lambda i: (i, 0)),
            pl.BlockSpec((TOP_K, tm), lambda i: (0, i)),
            pl.BlockSpec((TOP_K, tm), lambda i: (0, i)),
        ],
        out_shape=[
            jax.ShapeDtypeStruct((T, D), F32),
            jax.ShapeDtypeStruct((TOP_K, T), I32),
            jax.ShapeDtypeStruct((TOP_K, T), F32),
        ],
        compiler_params=_params("parallel"),
        name="mix_router",
    )(ya, yc, x2d, gain, w_o, g, b, wr_hi, wr_lo, b_r)


def _rank_kernel(idx_ref, tri_ref, rank_ref, cnt_ref, carry):
    tt = idx_ref.shape[1]

    @pl.when(pl.program_id(0) == 0)
    def _init():
        carry[...] = jnp.zeros_like(carry)

    eidx = lax.broadcasted_iota(I32, (N_EXPERTS, tt), 0)
    idx = idx_ref[...]
    onehots = [eidx == idx[k:k + 1, :] for k in range(TOP_K)]
    sel = jnp.zeros((N_EXPERTS, tt), F32)
    for oh in onehots:
        sel = sel + jnp.where(oh, 1.0, 0.0)
    prefix = _dot(sel.astype(BF16), tri_ref[...]) + carry[:, 0:1]
    ranks = [jnp.sum(jnp.where(oh, prefix, 0.0), axis=0, keepdims=True) for oh in onehots]
    rank_ref[...] = jnp.concatenate(ranks, axis=0).astype(I32)
    carry[...] = carry[...] + jnp.sum(sel, axis=1, keepdims=True)
    cnt_ref[...] = carry[...]


def _expert_ranks(top_idx_t, tri, tt):
    T = top_idx_t.shape[1]
    return pl.pallas_call(
        _rank_kernel,
        grid=(T // tt,),
        in_specs=[
            pl.BlockSpec((TOP_K, tt), lambda i: (0, i)),
            pl.BlockSpec((tt, tt), lambda i: (0, 0)),
        ],
        out_specs=[
            pl.BlockSpec((TOP_K, tt), lambda i: (0, i)),
            pl.BlockSpec((N_EXPERTS, LANES), lambda i: (0, 0)),
        ],
        out_shape=[
            jax.ShapeDtypeStruct((TOP_K, T), I32),
            jax.ShapeDtypeStruct((N_EXPERTS, LANES), F32),
        ],
        scratch_shapes=[pltpu.VMEM((N_EXPERTS, LANES), F32)],
        compiler_params=_params("arbitrary"),
        name="expert_ranks",
    )(top_idx_t, tri)


def _dispatch_kernel(zrow_ref, dest_ref, x_ref, xs_ref, zbuf, zsem, sem):
    tt = x_ref.shape[0]

    def zero_copy(e):
        r = pl.multiple_of(jnp.maximum(zrow_ref[e], 0), MOE_BLOCK)
        return pltpu.make_async_copy(zbuf, xs_ref.at[pl.ds(r, MOE_BLOCK), :], zsem)

    @pl.when(pl.program_id(0) == 0)
    def _zero_padding_rows():
        zbuf[...] = jnp.zeros_like(zbuf)
        for e in range(2 * N_EXPERTS):
            @pl.when(zrow_ref[e] >= 0)
            def _():
                zero_copy(e).start()
        for e in range(2 * N_EXPERTS):
            @pl.when(zrow_ref[e] >= 0)
            def _():
                zero_copy(e).wait()

    def row_copy(t, d):
        return pltpu.make_async_copy(x_ref.at[pl.ds(t, 1), :], xs_ref.at[pl.ds(d, 1), :], sem)

    def issue(t, c):
        for k in range(TOP_K):
            row_copy(t, dest_ref[k, t]).start()
        return c

    lax.fori_loop(0, tt, issue, 0)

    def drain(t, c):
        for k in range(TOP_K):
            row_copy(t, dest_ref[k, t]).wait()
        return c

    lax.fori_loop(0, tt, drain, 0)


def _dispatch(zrow, dest_t, x1, n_rows, tt):
    T, D = x1.shape
    return pl.pallas_call(
        _dispatch_kernel,
        grid_spec=pltpu.PrefetchScalarGridSpec(
            num_scalar_prefetch=1,
            grid=(T // tt,),
            in_specs=[
                pl.BlockSpec((TOP_K, tt), lambda i, z: (0, i), memory_space=pltpu.SMEM),
                pl.BlockSpec((tt, D), lambda i, z: (i, 0)),
            ],
            out_specs=pl.BlockSpec(memory_space=pl.ANY),
            scratch_shapes=[
                pltpu.VMEM((MOE_BLOCK, D), F32),
                pltpu.SemaphoreType.DMA(()),
                pltpu.SemaphoreType.DMA(()),
            ],
        ),
        out_shape=jax.ShapeDtypeStruct((n_rows, D), F32),
        compiler_params=_params("arbitrary"),
        name="dispatch",
    )(zrow, dest_t, x1)


def _moe_kernel(be_ref, nu_ref, xs_ref, wgu_ref, bgu_ref, wdn_ref, bdn_ref, y_ref):
    d_ff = wdn_ref.shape[1]

    @pl.when(pl.program_id(0) < nu_ref[0])
    def _():
        h = _dot(xs_ref[...].astype(BF16), wgu_ref[0]) + bgu_ref[0]
        x_glu = jnp.minimum(h[:, :d_ff], SWIGLU_LIMIT)
        x_lin = jnp.clip(h[:, d_ff:], -SWIGLU_LIMIT, SWIGLU_LIMIT)
        act = x_glu * jax.nn.sigmoid(SWIGLU_ALPHA * x_glu) * (x_lin + 1.0)
        y_ref[...] = _dot(act.astype(BF16), wdn_ref[0]) + bdn_ref[0]

    @pl.when(pl.program_id(0) >= nu_ref[0])
    def _():
        y_ref[...] = jnp.zeros_like(y_ref)


def _moe_blocks(block_e, n_used, xs, w_gu, b_gu, w_dn, b_dn):
    n_rows, D = xs.shape
    n_blocks = n_rows // MOE_BLOCK
    d_ff = w_dn.shape[1]
    blk = lambda g, be, nu: (jnp.minimum(g, nu[0] - 1), 0)
    exp3 = lambda g, be, nu: (be[g], 0, 0)
    return pl.pallas_call(
        _moe_kernel,
        grid_spec=pltpu.PrefetchScalarGridSpec(
            num_scalar_prefetch=2,
            grid=(n_blocks,),
            in_specs=[
                pl.BlockSpec((MOE_BLOCK, D), blk),
                pl.BlockSpec((1, D, 2 * d_ff), exp3),
                pl.BlockSpec((1, 1, 2 * d_ff), exp3),
                pl.BlockSpec((1, d_ff, D), exp3),
                pl.BlockSpec((1, 1, D), exp3),
            ],
            out_specs=pl.BlockSpec((MOE_BLOCK, D), lambda g, be, nu: (g, 0)),
        ),
        out_shape=jax.ShapeDtypeStruct((n_rows, D), F32),
        compiler_params=_params("arbitrary"),
        name="moe_blocks",
    )(block_e, n_used, xs, w_gu, b_gu, w_dn, b_dn)


def _combine_kernel(dest_ref, gate_ref, x1_ref, g_ref, b_ref, y_hbm, o_ref, ybuf, sem):
    tt = x1_ref.shape[0]

    def row_copy(k, t, d):
        return pltpu.make_async_copy(y_hbm.at[pl.ds(d, 1), :], ybuf.at[k, pl.ds(t, 1), :], sem.at[k])

    def issue(t, c):
        for k in range(TOP_K):
            row_copy(k, t, dest_ref[k, t]).start()
        return c

    lax.fori_loop(0, tt, issue, 0)

    def drain(t, c):
        for k in range(TOP_K):
            row_copy(k, t, dest_ref[k, t]).wait()
        return c

    lax.fori_loop(0, tt, drain, 0)

    gates = gate_ref[...]
    y = gates[:, 0:1] * ybuf[0]
    for k in range(1, TOP_K):
        y = y + gates[:, k:k + 1] * ybuf[k]
    o_ref[...] = _layer_norm(DEEPNORM_ALPHA * x1_ref[...] + y, g_ref[...], b_ref[...])


def _combine(dest_t, gates, x1, g, b, y_buf, tt):
    T, D = x1.shape
    return pl.pallas_call(
        _combine_kernel,
        grid=(T // tt,),
        in_specs=[
            pl.BlockSpec((TOP_K, tt), lambda i: (0, i), memory_space=pltpu.SMEM),
            pl.BlockSpec((tt, TOP_K), lambda i: (i, 0)),
            pl.BlockSpec((tt, D), lambda i: (i, 0)),
            pl.BlockSpec((1, D), lambda i: (0, 0)),
            pl.BlockSpec((1, D), lambda i: (0, 0)),
            pl.BlockSpec(memory_space=pl.ANY),
        ],
        out_specs=pl.BlockSpec((tt, D), lambda i: (i, 0)),
        out_shape=jax.ShapeDtypeStruct((T, D), F32),
        scratch_shapes=[pltpu.VMEM((TOP_K, tt, D), F32), pltpu.SemaphoreType.DMA((TOP_K,))],
        compiler_params=_params("arbitrary"),
        name="combine",
    )(dest_t, gates, x1, g, b, y_buf)


def _rotary_lane_tables(positions):
    half = ROT_DIM // 2
    inv_freq = ROPE_THETA ** (-jnp.arange(0, ROT_DIM, 2, dtype=F32) / ROT_DIM)
    ang = positions.reshape(-1).astype(F32)[:, None] * inv_freq
    cos, sin = jnp.cos(ang), jnp.sin(ang)
    r = jnp.arange(LANES) % HEAD_DIM
    first = (r < half)[None, :]
    second = ((r >= half) & (r < ROT_DIM))[None, :]
    cos_l = jnp.take(cos, r % half, axis=1)
    sin_l = jnp.take(sin, r % half, axis=1)
    c = jnp.where(first | second, cos_l, 1.0)
    s1 = jnp.where(first, -sin_l, 0.0)
    s2 = jnp.where(second, sin_l, 0.0)
    return c, s1, s2


def _route_metadata(counts, n_blocks):
    counts = counts.astype(I32)
    padded = ((counts + MOE_BLOCK - 1) // MOE_BLOCK) * MOE_BLOCK
    pad_end = jnp.cumsum(padded)
    pad_start = pad_end - padded
    n_used = pad_end[-1] // MOE_BLOCK
    g = jnp.minimum(jnp.arange(n_blocks, dtype=I32), n_used - 1)
    block_e = jnp.clip(jnp.searchsorted(pad_end, g * MOE_BLOCK, side="right"), 0, N_EXPERTS - 1).astype(I32)
    tail = n_used + jnp.arange(N_EXPERTS, dtype=I32)
    zrow = jnp.concatenate([jnp.where(padded > 0, pad_end - MOE_BLOCK, -1),
                            jnp.where(tail < n_blocks, tail * MOE_BLOCK, -1)]).astype(I32)
    return pad_start, block_e, n_used.reshape(1).astype(I32), zrow


def kernel(x, positions, w_in, attn_gain, conv_w, conv_b, conv_ln_g, conv_ln_b, w_o, ln1_g, ln1_b,
           w_router, b_router, w_gu, b_gu, w_dn, b_dn, ln2_g, ln2_b):
    B, S, D = x.shape
    T = B * S
    depth = w_in.shape[0]
    tm = 512
    tt_rank = 512
    tt_move = 256
    n_blocks = T * TOP_K // MOE_BLOCK + N_EXPERTS
    n_rows = n_blocks * MOE_BLOCK

    rot_c, rot_s1, rot_s2 = _rotary_lane_tables(positions)
    tri = (jnp.arange(tt_rank)[:, None] < jnp.arange(tt_rank)[None, :]).astype(BF16)
    row = lambda a: a.reshape(1, -1)

    x2d = x.reshape(T, D)
    for l in range(depth):
        w_qkv = w_in[l, :, :3 * ATTN_WIDTH].astype(BF16)
        w_cv = w_in[l, :, 3 * ATTN_WIDTH:].astype(BF16)
        qkv = _qkv_proj(x2d, w_qkv, rot_c, rot_s1, rot_s2, tm)
        ya = _moba_attention(qkv.reshape(B, S, 3 * ATTN_WIDTH)).reshape(T, ATTN_WIDTH)
        yc = _conv_branch(x2d.reshape(B, S, D), w_cv, conv_w[l], row(conv_b[l]), row(conv_ln_g[l]),
                          row(conv_ln_b[l]), tm).reshape(T, CONV_WIDTH)
        wr_hi, wr_lo = _split_bf16(w_router[l].T)
        x1, top_idx_t, gates_t = _mix_router(
            ya, yc, x2d, row(attn_gain[l]), w_o[l].astype(BF16), row(ln1_g[l]), row(ln1_b[l]),
            wr_hi, wr_lo, b_router[l].reshape(N_EXPERTS, 1), tm)
        rank_t, counts = _expert_ranks(top_idx_t, tri, tt_rank)
        pad_start, block_e, n_used, zrow = _route_metadata(counts[:, 0], n_blocks)
        dest_t = rank_t + pad_start[top_idx_t]
        xs = _dispatch(zrow, dest_t, x1, n_rows, tt_move)
        y_buf = _moe_blocks(block_e, n_used, xs, w_gu[l].astype(BF16), b_gu[l][:, None, :],
                            w_dn[l].astype(BF16), b_dn[l][:, None, :])
        x2d = _combine(dest_t, gates_t.T, x1, row(ln2_g[l]), row(ln2_b[l]), y_buf, tt_move)
    return x2d.reshape(B, S, D)
```

```python
import functools

import jax
import jax.numpy as jnp
from jax import lax
from jax.experimental import pallas as pl
from jax.experimental.pallas import tpu as pltpu

F32 = jnp.float32
BF16 = jnp.bfloat16
I32 = jnp.int32

HEAD_DIM = 64
ATTN_HEADS = 8
ATTN_WIDTH = ATTN_HEADS * HEAD_DIM
CONV_WIDTH = 512
ROT_DIM = HEAD_DIM // 4
ROPE_THETA = 500000.0
MOBA_BLOCK = 256
MOBA_TOPK = 3
CONV_KERNEL = 31
N_EXPERTS = 32
TOP_K = 4
MOE_BLOCK = 256
SWIGLU_LIMIT = 7.0
SWIGLU_ALPHA = 1.702
DEEPNORM_ALPHA = (2.0 * 4) ** 0.25
LN_EPS = 1e-5

LANES = 128
CONV_HALO = 32
MASK_PENALTY = -1e30
VMEM_LIMIT = 48 * 1024 * 1024

_NT = (((1,), (1,)), ((), ()))


def _params(*sem):
    return pltpu.CompilerParams(dimension_semantics=sem, vmem_limit_bytes=VMEM_LIMIT)


def _dot(a, b):
    return jnp.dot(a, b, preferred_element_type=F32)


def _dot_nt(a, b):
    return lax.dot_general(a, b, _NT, preferred_element_type=F32)


def _split_bf16(a):
    hi = a.astype(BF16)
    lo = (a - hi.astype(F32)).astype(BF16)
    return hi, lo


def _layer_norm(z, g, b):
    mu = jnp.mean(z, axis=-1, keepdims=True)
    zc = z - mu
    var = jnp.mean(zc * zc, axis=-1, keepdims=True)
    return zc * lax.rsqrt(var + LN_EPS) * g + b


def _qkv_kernel(x_ref, w32_ref, c_ref, s1_ref, s2_ref, o_ref, w_ref):
    @pl.when(pl.program_id(0) == 0)
    def _cast_weights():
        w_ref[...] = w32_ref[...].astype(BF16)

    xb = x_ref[...].astype(BF16)
    reps = ATTN_WIDTH // LANES
    c = jnp.concatenate([c_ref[...]] * reps, axis=1)
    s1 = jnp.concatenate([s1_ref[...]] * reps, axis=1)
    s2 = jnp.concatenate([s2_ref[...]] * reps, axis=1)
    half = ROT_DIM // 2
    for part in range(3):
        cols = slice(part * ATTN_WIDTH, (part + 1) * ATTN_WIDTH)
        h = _dot(xb, w_ref[:, cols])
        if part < 2:
            h = h * c + pltpu.roll(h, ATTN_WIDTH - half, 1) * s1 + pltpu.roll(h, half, 1) * s2
        o_ref[:, cols] = h.astype(BF16)


def _qkv_proj(x2d, w_in, layer, rot_c, rot_s1, rot_s2, tm):
    T, D = x2d.shape
    N = 3 * ATTN_WIDTH
    return pl.pallas_call(
        _qkv_kernel,
        grid=(T // tm,),
        in_specs=[
            pl.BlockSpec((tm, D), lambda i: (i, 0)),
            pl.BlockSpec((None, D, N), lambda i: (layer, 0, 0)),
            pl.BlockSpec((tm, LANES), lambda i: (i, 0)),
            pl.BlockSpec((tm, LANES), lambda i: (i, 0)),
            pl.BlockSpec((tm, LANES), lambda i: (i, 0)),
        ],
        out_specs=pl.BlockSpec((tm, N), lambda i: (i, 0)),
        out_shape=jax.ShapeDtypeStruct((T, N), BF16),
        scratch_shapes=[pltpu.VMEM((D, N), BF16)],
        compiler_params=_params("arbitrary"),
        name="qkv_proj",
    )(x2d, w_in, rot_c, rot_s1, rot_s2)


def _moba_kernel(q_ref, k_ref, v_ref, o_ref, kaug_ref, s_ref):
    S = k_ref.shape[1]
    nb = S // MOBA_BLOCK
    BQ = MOBA_BLOCK
    scale = HEAD_DIM ** -0.5

    kf = k_ref[0].astype(F32)
    kmean = jnp.sum(kf.reshape(nb, MOBA_BLOCK, LANES), axis=1) / float(MOBA_BLOCK)
    lane_s = lax.broadcasted_iota(I32, (S, LANES), 1)
    blk = lax.broadcasted_iota(I32, (S, LANES), 0) // MOBA_BLOCK
    lane8 = lax.broadcasted_iota(I32, (nb, LANES), 1)
    km = []
    for h in range(2):
        own_s = (lane_s >= h * HEAD_DIM) & (lane_s < (h + 1) * HEAD_DIM)
        spare = (1 - h) * HEAD_DIM
        ind = jnp.where(lane_s == spare + blk, 1.0, 0.0)
        kaug_ref[h] = jnp.where(own_s, kf, ind).astype(BF16)
        own8 = (lane8 >= h * HEAD_DIM) & (lane8 < (h + 1) * HEAD_DIM)
        km.append(_split_bf16(jnp.where(own8, kmean, 0.0)))

    lane = lax.broadcasted_iota(I32, (BQ, LANES), 1)
    jidx = lax.broadcasted_iota(I32, (nb, BQ), 0)
    causal = lax.broadcasted_iota(I32, (BQ, BQ), 1) <= lax.broadcasted_iota(I32, (BQ, BQ), 0)

    for qb in range(nb):
        rows = slice(qb * BQ, (qb + 1) * BQ)
        q16 = q_ref[0, rows, :]
        qf = q16.astype(F32)
        outs = []
        for h in range(2):
            own = (lane >= h * HEAD_DIM) & (lane < (h + 1) * HEAD_DIM)
            spare = (1 - h) * HEAD_DIM
            qs = jnp.where(own, qf * scale, 0.0)
            if qb > MOBA_TOPK:
                gate = _dot_nt(km[h][0], q16) + _dot_nt(km[h][1], q16)
                gm = jnp.where(jidx < qb, gate, -jnp.inf)
                rank = jnp.zeros((nb, BQ), F32)
                for j2 in range(qb):
                    other = gm[j2:j2 + 1, :]
                    beats = (other > gm) | ((other == gm) & (j2 < jidx))
                    rank = rank + jnp.where(beats, 1.0, 0.0)
                keep = (rank < float(MOBA_TOPK)) | (jidx >= qb)
                pen_t = jnp.where(keep, 0.0, MASK_PENALTY)
                pieces = [pen_t, jnp.zeros((LANES - spare - nb, BQ), F32)]
                if spare > 0:
                    pieces.insert(0, jnp.zeros((spare, BQ), F32))
                q_aug = (qs + jnp.concatenate(pieces, axis=0).T).astype(BF16)
            else:
                q_aug = qs.astype(BF16)

            mt = None
            for j in range(qb + 1):
                cols = slice(j * BQ, (j + 1) * BQ)
                s = _dot_nt(q_aug, kaug_ref[h, cols, :])
                if j == qb:
                    s = jnp.where(causal, s, -jnp.inf)
                s_ref[h, :, cols] = s
                t = jnp.maximum(s[:, :LANES], s[:, LANES:])
                mt = t if mt is None else jnp.maximum(mt, t)
            m = jnp.max(mt, axis=-1, keepdims=True)
            lt = None
            acc = None
            for j in range(qb + 1):
                cols = slice(j * BQ, (j + 1) * BQ)
                p = jnp.exp(s_ref[h, :, cols] - m)
                pt = p[:, :LANES] + p[:, LANES:]
                lt = pt if lt is None else lt + pt
                pv = _dot(p.astype(BF16), v_ref[0, cols, :])
                acc = pv if acc is None else acc + pv
            outs.append(acc / jnp.sum(lt, axis=-1, keepdims=True))
        o_ref[0, rows, :] = jnp.where(lane < HEAD_DIM, outs[0], outs[1])


def _moba_attention(qkv3):
    B, S, _ = qkv3.shape
    n_pairs = ATTN_WIDTH // LANES
    return pl.pallas_call(
        _moba_kernel,
        grid=(B, n_pairs),
        in_specs=[
            pl.BlockSpec((1, S, LANES), lambda b, p: (b, 0, p)),
            pl.BlockSpec((1, S, LANES), lambda b, p: (b, 0, n_pairs + p)),
            pl.BlockSpec((1, S, LANES), lambda b, p: (b, 0, 2 * n_pairs + p)),
        ],
        out_specs=pl.BlockSpec((1, S, LANES), lambda b, p: (b, 0, p)),
        out_shape=jax.ShapeDtypeStruct((B, S, ATTN_WIDTH), F32),
        scratch_shapes=[pltpu.VMEM((2, S, LANES), BF16), pltpu.VMEM((2, MOBA_BLOCK, S), F32)],
        compiler_params=_params("parallel", "parallel"),
        name="moba_attention",
    )(qkv3, qkv3, qkv3)


def _conv_kernel(x_ref, wa32_ref, wg32_ref, cw_ref, cb_ref, g_ref, b_ref, o_ref, ubuf, w_ref):
    si = pl.program_id(1)
    tm = x_ref.shape[1]

    @pl.when((pl.program_id(0) == 0) & (si == 0))
    def _cast_weights():
        w_ref[:, :CONV_WIDTH] = wa32_ref[...].astype(BF16)
        w_ref[:, CONV_WIDTH:] = wg32_ref[...].astype(BF16)

    @pl.when(si == 0)
    def _zero_history():
        ubuf[0:CONV_HALO, :] = jnp.zeros((CONV_HALO, CONV_WIDTH), F32)

    @pl.when(si > 0)
    def _carry_history():
        ubuf[0:CONV_HALO, :] = ubuf[tm:tm + CONV_HALO, :]

    u2 = _dot(x_ref[0].astype(BF16), w_ref[...])
    ubuf[CONV_HALO:CONV_HALO + tm, :] = u2[:, :CONV_WIDTH] * jax.nn.sigmoid(u2[:, CONV_WIDTH:])
    acc = jnp.broadcast_to(cb_ref[...], (tm, CONV_WIDTH))
    base = CONV_HALO - (CONV_KERNEL - 1)
    for j in range(CONV_KERNEL):
        acc = acc + cw_ref[j:j + 1, :] * ubuf[base + j:base + j + tm, :]
    y = _layer_norm(acc, g_ref[...], b_ref[...])
    o_ref[0] = (y * jax.nn.sigmoid(y)).astype(BF16)


def _conv_branch(x3, w_in, layer, cw, cb, g, b, tm):
    B, S, D = x3.shape
    first = 3 * ATTN_WIDTH // CONV_WIDTH
    return pl.pallas_call(
        _conv_kernel,
        grid=(B, S // tm),
        in_specs=[
            pl.BlockSpec((1, tm, D), lambda bb, i: (bb, i, 0)),
            pl.BlockSpec((None, D, CONV_WIDTH), lambda bb, i: (layer, 0, first)),
            pl.BlockSpec((None, D, CONV_WIDTH), lambda bb, i: (layer, 0, first + 1)),
            pl.BlockSpec((CONV_KERNEL, CONV_WIDTH), lambda bb, i: (0, 0)),
            pl.BlockSpec((1, CONV_WIDTH), lambda bb, i: (0, 0)),
            pl.BlockSpec((1, CONV_WIDTH), lambda bb, i: (0, 0)),
            pl.BlockSpec((1, CONV_WIDTH), lambda bb, i: (0, 0)),
        ],
        out_specs=pl.BlockSpec((1, tm, CONV_WIDTH), lambda bb, i: (bb, i, 0)),
        out_shape=jax.ShapeDtypeStruct((B, S, CONV_WIDTH), BF16),
        scratch_shapes=[pltpu.VMEM((CONV_HALO + tm, CONV_WIDTH), F32), pltpu.VMEM((D, 2 * CONV_WIDTH), BF16)],
        compiler_params=_params("arbitrary", "arbitrary"),
        name="conv_branch",
    )(x3, w_in, w_in, cw, cb, g, b)


def _mix_kernel(ya_ref, yc_ref, x_ref, gain_ref, wo32_ref, g_ref, b_ref, wrh_ref, wrl_ref, br_ref,
                x1_ref, idx_ref, gate_ref, wo_ref):
    tm = x_ref.shape[0]

    @pl.when(pl.program_id(0) == 0)
    def _cast_weights():
        wo_ref[...] = wo32_ref[...].astype(BF16)

    ya = ya_ref[...]
    ms = jnp.mean(ya * ya, axis=-1, keepdims=True)
    yan = ya * lax.rsqrt(ms + LN_EPS) * gain_ref[...]
    mix = _dot(yan.astype(BF16), wo_ref[0:ATTN_WIDTH, :]) + _dot(yc_ref[...], wo_ref[ATTN_WIDTH:, :])
    x1 = _layer_norm(DEEPNORM_ALPHA * x_ref[...] + mix, g_ref[...], b_ref[...])
    x1_ref[...] = x1
    xh, xl = _split_bf16(x1)
    wrh = wrh_ref[...]
    logit = _dot_nt(wrh, xh) + _dot_nt(wrl_ref[...], xh) + _dot_nt(wrh, xl) + br_ref[...]
    eidx = lax.broadcasted_iota(I32, (N_EXPERTS, tm), 0)
    vals, idxs = [], []
    for _ in range(TOP_K):
        m = jnp.max(logit, axis=0, keepdims=True)
        idx = jnp.min(jnp.where(logit == m, eidx, N_EXPERTS), axis=0, keepdims=True)
        vals.append(m)
        idxs.append(idx)
        logit = jnp.where(eidx == idx, -jnp.inf, logit)
    es = [jnp.exp(v - vals[0]) for v in vals]
    tot = es[0] + es[1] + es[2] + es[3]
    idx_ref[...] = jnp.concatenate(idxs, axis=0)
    gate_ref[...] = jnp.concatenate([e / tot for e in es], axis=0)


def _mix_router(ya, yc, x2d, gain, w_o, layer, g, b, wr_hi, wr_lo, b_r, tm):
    T, D = x2d.shape
    full = lambda shape: pl.BlockSpec(shape, lambda i: (0, 0))
    return pl.pallas_call(
        _mix_kernel,
        grid=(T // tm,),
        in_specs=[
            pl.BlockSpec((tm, ATTN_WIDTH), lambda i: (i, 0)),
            pl.BlockSpec((tm, CONV_WIDTH), lambda i: (i, 0)),
            pl.BlockSpec((tm, D), lambda i: (i, 0)),
            full((1, ATTN_WIDTH)),
            pl.BlockSpec((None, D, D), lambda i: (layer, 0, 0)),
            full((1, D)),
            full((1, D)),
            full((N_EXPERTS, D)),
            full((N_EXPERTS, D)),
            full((N_EXPERTS, 1)),
        ],
        out_specs=[
            pl.BlockSpec((tm, D), lambda i: (i, 0)),
            pl.BlockSpec((TOP_K, tm), lambda i: (0, i)),
            pl.BlockSpec((TOP_K, tm), lambda i: (0, i)),
        ],
        out_shape=[
            jax.ShapeDtypeStruct((T, D), F32),
            jax.ShapeDtypeStruct((TOP_K, T), I32),
            jax.ShapeDtypeStruct((TOP_K, T), F32),
        ],
        scratch_shapes=[pltpu.VMEM((D, D), BF16)],
        compiler_params=_params("arbitrary"),
        name="mix_router",
    )(ya, yc, x2d, gain, w_o, g, b, wr_hi, wr_lo, b_r)


def _rank_kernel(idx_ref, tri_ref, low_ref, pos_ref, cnt_ref, before_ref, carry):
    tt = idx_ref.shape[1]

    @pl.when(pl.program_id(0) == 0)
    def _init():
        carry[...] = jnp.zeros_like(carry)

    eidx = lax.broadcasted_iota(I32, (N_EXPERTS, tt), 0)
    idx = idx_ref[...]
    onehots = [eidx == idx[k:k + 1, :] for k in range(TOP_K)]
    sel = jnp.zeros((N_EXPERTS, tt), F32)
    for oh in onehots:
        sel = sel + jnp.where(oh, 1.0, 0.0)
    prefix = _dot(sel.astype(BF16), tri_ref[...])
    cnt = jnp.broadcast_to(jnp.sum(sel, axis=1, keepdims=True), (N_EXPERTS, LANES))
    start = _dot(low_ref[...], cnt.astype(BF16))
    local = prefix + start[:, 0:1]
    pos = [jnp.sum(jnp.where(oh, local, 0.0), axis=0, keepdims=True) for oh in onehots]
    pos_ref[...] = jnp.concatenate(pos, axis=0).astype(I32)
    cnt_ref[...] = cnt
    before_ref[...] = carry[...]
    carry[...] = carry[...] + cnt


def _tile_sort_positions(top_idx_t, tt):
    T = top_idx_t.shape[1]
    assert tt <= 256, "per-tile expert counts must stay exactly representable in bf16"
    tri = (jnp.arange(tt)[:, None] < jnp.arange(tt)[None, :]).astype(BF16)
    low = (jnp.arange(N_EXPERTS)[None, :] < jnp.arange(N_EXPERTS)[:, None]).astype(BF16)
    per_tile = pl.BlockSpec((None, N_EXPERTS, LANES), lambda i: (i, 0, 0))
    return pl.pallas_call(
        _rank_kernel,
        grid=(T // tt,),
        in_specs=[
            pl.BlockSpec((TOP_K, tt), lambda i: (0, i)),
            pl.BlockSpec((tt, tt), lambda i: (0, 0)),
            pl.BlockSpec((N_EXPERTS, N_EXPERTS), lambda i: (0, 0)),
        ],
        out_specs=[pl.BlockSpec((TOP_K, tt), lambda i: (0, i)), per_tile, per_tile],
        out_shape=[
            jax.ShapeDtypeStruct((TOP_K, T), I32),
            jax.ShapeDtypeStruct((T // tt, N_EXPERTS, LANES), F32),
            jax.ShapeDtypeStruct((T // tt, N_EXPERTS, LANES), F32),
        ],
        scratch_shapes=[pltpu.VMEM((N_EXPERTS, LANES), F32)],
        compiler_params=_params("arbitrary"),
        name="tile_sort_positions",
    )(top_idx_t, tri, low)


SEGMENT_CHUNKS = (256, 128, 64, 32, 16, 8, 4, 2, 1)


def _start_segment_copies(n_rows, make_copy):
    for p in SEGMENT_CHUNKS:
        offset = n_rows & ~(2 * p - 1)

        @pl.when((n_rows & p) != 0)
        def _():
            make_copy(offset, p).start()


ROW_TILES = 8


def _token_rows(start, n):
    return pl.ds(pl.multiple_of(start * ROW_TILES, ROW_TILES), n * ROW_TILES)


def _store_token_major(ref, val):
    n = val.shape[0]
    for c in range(ROW_TILES):
        ref[pl.ds(c, n, stride=ROW_TILES), :] = val[:, c * LANES:(c + 1) * LANES]


def _load_token_major(ref, n):
    return jnp.concatenate([ref[pl.ds(c, n, stride=ROW_TILES), :] for c in range(ROW_TILES)], axis=1)


def _dispatch_kernel(zrow_ref, src_ref, len_ref, dst_ref, pos_ref, x_ref, xs_ref, sorted_ref, zbuf, zsem, sem):
    i = pl.program_id(0)
    n_tiles = pl.num_programs(0)
    tt = x_ref.shape[0]
    A = TOP_K * tt
    slot = i % 2

    def zero_copy(e):
        r = pl.multiple_of(jnp.maximum(zrow_ref[e], 0), MOE_BLOCK)
        return pltpu.make_async_copy(zbuf, xs_ref.at[_token_rows(r, MOE_BLOCK), :], zsem)

    @pl.when(pl.program_id(0) == 0)
    def _zero_padding_rows():
        zbuf[...] = jnp.zeros_like(zbuf)
        for e in range(2 * N_EXPERTS):
            @pl.when(zrow_ref[e] >= 0)
            def _():
                zero_copy(e).start()
        for e in range(2 * N_EXPERTS):
            @pl.when(zrow_ref[e] >= 0)
            def _():
                zero_copy(e).wait()

    def drain(s):
        pltpu.make_async_copy(sorted_ref.at[s], xs_ref.at[_token_rows(0, A), :], sem.at[s]).wait()

    @pl.when(i >= 2)
    def _slot_is_free():
        drain(slot)

    r = lax.broadcasted_iota(I32, (A, tt), 0)
    pos = pos_ref[...]
    onehot = jnp.zeros((A, tt), F32)
    for k in range(TOP_K):
        onehot = jnp.where(r == pos[k:k + 1, :], 1.0, onehot)
    _store_token_major(sorted_ref.at[slot], _dot(onehot.astype(BF16), x_ref[...].astype(BF16)))

    for e in range(N_EXPERTS):
        seg = i * N_EXPERTS + e
        src, dst = src_ref[seg], dst_ref[seg]
        _start_segment_copies(len_ref[seg], lambda off, p: pltpu.make_async_copy(
            sorted_ref.at[slot, _token_rows(src + off, p), :], xs_ref.at[_token_rows(dst + off, p), :],
            sem.at[slot]))

    @pl.when(i == n_tiles - 1)
    def _finish():
        drain(slot)

        @pl.when(i >= 1)
        def _():
            drain(1 - slot)


def _dispatch(zrow, seg_src, seg_len, seg_dst, pos_t, x1, n_rows, tt):
    T, D = x1.shape
    return pl.pallas_call(
        _dispatch_kernel,
        grid_spec=pltpu.PrefetchScalarGridSpec(
            num_scalar_prefetch=4,
            grid=(T // tt,),
            in_specs=[
                pl.BlockSpec((TOP_K, tt), lambda i, *_: (0, i)),
                pl.BlockSpec((tt, D), lambda i, *_: (i, 0)),
            ],
            out_specs=pl.BlockSpec(memory_space=pl.ANY),
            scratch_shapes=[
                pltpu.VMEM((2, TOP_K * tt * ROW_TILES, LANES), F32),
                pltpu.VMEM((MOE_BLOCK * ROW_TILES, LANES), F32),
                pltpu.SemaphoreType.DMA(()),
                pltpu.SemaphoreType.DMA((2,)),
            ],
        ),
        out_shape=jax.ShapeDtypeStruct((n_rows * ROW_TILES, LANES), F32),
        compiler_params=_params("arbitrary"),
        name="dispatch",
    )(zrow, seg_src, seg_len, seg_dst, pos_t, x1)


def _moe_kernel(be_ref, nu_ref, xs_ref, wgu32_ref, bgu_ref, wdn32_ref, bdn_ref, y_ref, wgu_ref, wdn_ref):
    d_ff = wdn_ref.shape[0]
    g = pl.program_id(0)

    @pl.when(g < nu_ref[0])
    def _():
        @pl.when((g == 0) | (be_ref[g] != be_ref[jnp.maximum(g - 1, 0)]))
        def _cast_expert_weights():
            wgu_ref[...] = wgu32_ref[0].astype(BF16)
            wdn_ref[...] = wdn32_ref[0].astype(BF16)

        h = _dot(_load_token_major(xs_ref, MOE_BLOCK).astype(BF16), wgu_ref[...]) + bgu_ref[0]
        x_glu = jnp.minimum(h[:, :d_ff], SWIGLU_LIMIT)
        x_lin = jnp.clip(h[:, d_ff:], -SWIGLU_LIMIT, SWIGLU_LIMIT)
        act = x_glu * jax.nn.sigmoid(SWIGLU_ALPHA * x_glu) * (x_lin + 1.0)
        _store_token_major(y_ref, _dot(act.astype(BF16), wdn_ref[...]) + bdn_ref[0])

    @pl.when(g >= nu_ref[0])
    def _():
        y_ref[...] = jnp.zeros_like(y_ref)


def _moe_blocks(block_e, n_used, xs, w_gu, b_gu, w_dn, b_dn, layer):
    D = w_gu.shape[2]
    assert D == ROW_TILES * LANES
    n_blocks = xs.shape[0] // (MOE_BLOCK * ROW_TILES)
    d_ff = w_dn.shape[2]
    blk = lambda g, be, nu: (jnp.minimum(g, nu[0] - 1), 0)
    exp3 = lambda g, be, nu: (be[g], 0, 0)
    exp4 = lambda g, be, nu: (layer, be[g], 0, 0)
    return pl.pallas_call(
        _moe_kernel,
        grid_spec=pltpu.PrefetchScalarGridSpec(
            num_scalar_prefetch=2,
            grid=(n_blocks,),
            in_specs=[
                pl.BlockSpec((MOE_BLOCK * ROW_TILES, LANES), blk),
                pl.BlockSpec((None, 1, D, 2 * d_ff), exp4),
                pl.BlockSpec((1, 1, 2 * d_ff), exp3),
                pl.BlockSpec((None, 1, d_ff, D), exp4),
                pl.BlockSpec((1, 1, D), exp3),
            ],
            out_specs=pl.BlockSpec((MOE_BLOCK * ROW_TILES, LANES), lambda g, be, nu: (g, 0)),
            scratch_shapes=[pltpu.VMEM((D, 2 * d_ff), BF16), pltpu.VMEM((d_ff, D), BF16)],
        ),
        out_shape=jax.ShapeDtypeStruct(xs.shape, F32),
        compiler_params=_params("arbitrary"),
        name="moe_blocks",
    )(block_e, n_used, xs, w_gu, b_gu, w_dn, b_dn)


def _combine_kernel(src_ref, len_ref, dst_ref, pos_ref, gate_ref, x1_ref, g_ref, b_ref, y_hbm, o_ref, ybuf, sem):
    i = pl.program_id(0)
    n_tiles = pl.num_programs(0)
    tt = x1_ref.shape[0]
    A = TOP_K * tt
    slot = i % 2

    def fetch(tile, s):
        for e in range(N_EXPERTS):
            seg = tile * N_EXPERTS + e
            src, dst = src_ref[seg], dst_ref[seg]
            _start_segment_copies(len_ref[seg], lambda off, p: pltpu.make_async_copy(
                y_hbm.at[_token_rows(dst + off, p), :], ybuf.at[s, _token_rows(src + off, p), :], sem.at[s]))

    @pl.when(i == 0)
    def _first_tile():
        fetch(0, 0)

    @pl.when(i + 1 < n_tiles)
    def _prefetch_next_tile():
        fetch(i + 1, 1 - slot)

    pltpu.make_async_copy(y_hbm.at[_token_rows(0, A), :], ybuf.at[slot], sem.at[slot]).wait()

    c = lax.broadcasted_iota(I32, (tt, A), 1)
    pos = pos_ref[...]
    gates = gate_ref[...]
    weights = jnp.zeros((tt, A), F32)
    for k in range(TOP_K):
        weights = jnp.where(c == pos[:, k:k + 1], gates[:, k:k + 1], weights)
    w_hi, w_lo = _split_bf16(weights)
    yl = _load_token_major(ybuf.at[slot], A).astype(BF16)
    y = _dot(w_hi, yl) + _dot(w_lo, yl)
    o_ref[...] = _layer_norm(DEEPNORM_ALPHA * x1_ref[...] + y, g_ref[...], b_ref[...])


def _combine(seg_src, seg_len, seg_dst, pos, gates, x1, g, b, y_buf, tt):
    T, D = x1.shape
    return pl.pallas_call(
        _combine_kernel,
        grid_spec=pltpu.PrefetchScalarGridSpec(
            num_scalar_prefetch=3,
            grid=(T // tt,),
            in_specs=[
                pl.BlockSpec((tt, TOP_K), lambda i, *_: (i, 0)),
                pl.BlockSpec((tt, TOP_K), lambda i, *_: (i, 0)),
                pl.BlockSpec((tt, D), lambda i, *_: (i, 0)),
                pl.BlockSpec((1, D), lambda i, *_: (0, 0)),
                pl.BlockSpec((1, D), lambda i, *_: (0, 0)),
                pl.BlockSpec(memory_space=pl.ANY),
            ],
            out_specs=pl.BlockSpec((tt, D), lambda i, *_: (i, 0)),
            scratch_shapes=[pltpu.VMEM((2, TOP_K * tt * ROW_TILES, LANES), F32), pltpu.SemaphoreType.DMA((2,))],
        ),
        out_shape=jax.ShapeDtypeStruct((T, D), F32),
        compiler_params=_params("arbitrary"),
        name="combine",
    )(seg_src, seg_len, seg_dst, pos, gates, x1, g, b, y_buf)


def _rotary_lane_tables(positions):
    half = ROT_DIM // 2
    inv_freq = ROPE_THETA ** (-jnp.arange(0, ROT_DIM, 2, dtype=F32) / ROT_DIM)
    ang = positions.reshape(-1).astype(F32)[:, None] * inv_freq
    cos, sin = jnp.cos(ang), jnp.sin(ang)
    r = jnp.arange(LANES) % HEAD_DIM
    first = (r < half)[None, :]
    second = ((r >= half) & (r < ROT_DIM))[None, :]
    cos_l = jnp.take(cos, r % half, axis=1)
    sin_l = jnp.take(sin, r % half, axis=1)
    c = jnp.where(first | second, cos_l, 1.0)
    s1 = jnp.where(first, -sin_l, 0.0)
    s2 = jnp.where(second, sin_l, 0.0)
    return c, s1, s2


def _route_metadata(tile_cnt, tile_before, n_blocks):
    tile_cnt = tile_cnt.astype(I32)
    tile_before = tile_before.astype(I32)
    counts = tile_before[-1] + tile_cnt[-1]
    padded = ((counts + MOE_BLOCK - 1) // MOE_BLOCK) * MOE_BLOCK
    pad_end = jnp.cumsum(padded)
    pad_start = pad_end - padded
    n_used = pad_end[-1] // MOE_BLOCK
    g = jnp.minimum(jnp.arange(n_blocks, dtype=I32), n_used - 1)
    block_e = jnp.sum((pad_end[None, :] <= (g * MOE_BLOCK)[:, None]).astype(I32), axis=1)
    block_e = jnp.clip(block_e, 0, N_EXPERTS - 1)
    tail = n_used + jnp.arange(N_EXPERTS, dtype=I32)
    zrow = jnp.concatenate([jnp.where(padded > 0, pad_end - MOE_BLOCK, -1),
                            jnp.where(tail < n_blocks, tail * MOE_BLOCK, -1)]).astype(I32)
    seg_src = (jnp.cumsum(tile_cnt, axis=1) - tile_cnt).reshape(-1)
    seg_dst = (pad_start[None, :] + tile_before).reshape(-1)
    return block_e, n_used.reshape(1).astype(I32), zrow, seg_src, tile_cnt.reshape(-1), seg_dst


def kernel(x, positions, w_in, attn_gain, conv_w, conv_b, conv_ln_g, conv_ln_b, w_o, ln1_g, ln1_b,
           w_router, b_router, w_gu, b_gu, w_dn, b_dn, ln2_g, ln2_b):
    B, S, D = x.shape
    T = B * S
    depth = w_in.shape[0]
    tm = 512
    tt = 256
    n_blocks = T * TOP_K // MOE_BLOCK + N_EXPERTS
    n_rows = n_blocks * MOE_BLOCK

    rot_c, rot_s1, rot_s2 = _rotary_lane_tables(positions)
    row = lambda a: a.reshape(1, -1)

    x2d = x.reshape(T, D)
    for l in range(depth):
        qkv = _qkv_proj(x2d, w_in, l, rot_c, rot_s1, rot_s2, tm)
        ya = _moba_attention(qkv.reshape(B, S, 3 * ATTN_WIDTH)).reshape(T, ATTN_WIDTH)
        yc = _conv_branch(x2d.reshape(B, S, D), w_in, l, conv_w[l], row(conv_b[l]), row(conv_ln_g[l]),
                          row(conv_ln_b[l]), tm).reshape(T, CONV_WIDTH)
        wr_hi, wr_lo = _split_bf16(w_router[l].T)
        x1, top_idx_t, gates_t = _mix_router(
            ya, yc, x2d, row(attn_gain[l]), w_o, l, row(ln1_g[l]), row(ln1_b[l]),
            wr_hi, wr_lo, b_router[l].reshape(N_EXPERTS, 1), tm)
        pos_t, tile_cnt, tile_before = _tile_sort_positions(top_idx_t, tt)
        block_e, n_used, zrow, seg_src, seg_len, seg_dst = _route_metadata(
            tile_cnt[:, :, 0], tile_before[:, :, 0], n_blocks)
        xs = _dispatch(zrow, seg_src, seg_len, seg_dst, pos_t, x1, n_rows, tt)
        y_buf = _moe_blocks(block_e, n_used, xs, w_gu, b_gu[l][:, None, :], w_dn, b_dn[l][:, None, :], l)
        x2d = _combine(seg_src, seg_len, seg_dst, pos_t.T, gates_t.T, x1, row(ln2_g[l]), row(ln2_b[l]),
                       y_buf, tt)
    return x2d.reshape(B, S, D)
```

```python
import functools

import jax
import jax.numpy as jnp
from jax import lax
from jax.experimental import pallas as pl
from jax.experimental.pallas import tpu as pltpu

F32 = jnp.float32
BF16 = jnp.bfloat16
I32 = jnp.int32

HEAD_DIM = 64
ATTN_HEADS = 8
ATTN_WIDTH = ATTN_HEADS * HEAD_DIM
CONV_WIDTH = 512
ROT_DIM = HEAD_DIM // 4
ROPE_THETA = 500000.0
MOBA_BLOCK = 256
MOBA_TOPK = 3
CONV_KERNEL = 31
N_EXPERTS = 32
TOP_K = 4
MOE_BLOCK = 256
MOE_SUB_BLOCKS = 2
MOE_SUPER = MOE_BLOCK * MOE_SUB_BLOCKS
SWIGLU_LIMIT = 7.0
SWIGLU_ALPHA = 1.702
DEEPNORM_ALPHA = (2.0 * 4) ** 0.25
LN_EPS = 1e-5

LANES = 128
SUBLANES = 8
CONV_ROWS = 64
CONV_HALO = 32
MASK_PENALTY = -1e30
VMEM_LIMIT = 48 * 1024 * 1024

_NT = (((1,), (1,)), ((), ()))


def _params(*sem):
    return pltpu.CompilerParams(dimension_semantics=sem, vmem_limit_bytes=VMEM_LIMIT)


def _dot(a, b):
    return jnp.dot(a, b, preferred_element_type=F32)


def _dot_nt(a, b):
    return lax.dot_general(a, b, _NT, preferred_element_type=F32)


def _split_bf16(a):
    hi = a.astype(BF16)
    lo = (a - hi.astype(F32)).astype(BF16)
    return hi, lo


def _layer_norm(z, g, b):
    mu = jnp.mean(z, axis=-1, keepdims=True)
    zc = z - mu
    var = jnp.mean(zc * zc, axis=-1, keepdims=True)
    return zc * lax.rsqrt(var + LN_EPS) * g + b


def _qkv_kernel(x_ref, w32_ref, c_ref, s1_ref, s2_ref, o_ref, w_ref):
    @pl.when(pl.program_id(0) == 0)
    def _cast_weights():
        w_ref[...] = w32_ref[...].astype(BF16)

    xb = x_ref[...].astype(BF16)
    reps = ATTN_WIDTH // LANES
    c = jnp.concatenate([c_ref[...]] * reps, axis=1)
    s1 = jnp.concatenate([s1_ref[...]] * reps, axis=1)
    s2 = jnp.concatenate([s2_ref[...]] * reps, axis=1)
    half = ROT_DIM // 2
    for part in range(3):
        cols = slice(part * ATTN_WIDTH, (part + 1) * ATTN_WIDTH)
        h = _dot(xb, w_ref[:, cols])
        if part < 2:
            h = h * c + pltpu.roll(h, ATTN_WIDTH - half, 1) * s1 + pltpu.roll(h, half, 1) * s2
        o_ref[:, cols] = h.astype(BF16)


def _qkv_proj(x2d, w_in, layer, rot_c, rot_s1, rot_s2, tm):
    T, D = x2d.shape
    N = 3 * ATTN_WIDTH
    return pl.pallas_call(
        _qkv_kernel,
        grid=(T // tm,),
        in_specs=[
            pl.BlockSpec((tm, D), lambda i: (i, 0)),
            pl.BlockSpec((None, D, N), lambda i: (layer, 0, 0)),
            pl.BlockSpec((tm, LANES), lambda i: (i, 0)),
            pl.BlockSpec((tm, LANES), lambda i: (i, 0)),
            pl.BlockSpec((tm, LANES), lambda i: (i, 0)),
        ],
        out_specs=pl.BlockSpec((tm, N), lambda i: (i, 0)),
        out_shape=jax.ShapeDtypeStruct((T, N), BF16),
        scratch_shapes=[pltpu.VMEM((D, N), BF16)],
        compiler_params=_params("arbitrary"),
        name="qkv_proj",
    )(x2d, w_in, rot_c, rot_s1, rot_s2)


def _moba_kernel(q_ref, k_ref, v_ref, o_ref, kaug_ref, s_ref):
    S = k_ref.shape[1]
    nb = S // MOBA_BLOCK
    BQ = MOBA_BLOCK
    scale = HEAD_DIM ** -0.5

    kf = k_ref[0].astype(F32)
    kmean = jnp.sum(kf.reshape(nb, MOBA_BLOCK, LANES), axis=1) / float(MOBA_BLOCK)
    lane_s = lax.broadcasted_iota(I32, (S, LANES), 1)
    blk = lax.broadcasted_iota(I32, (S, LANES), 0) // MOBA_BLOCK
    lane8 = lax.broadcasted_iota(I32, (nb, LANES), 1)
    km = []
    for h in range(2):
        own_s = (lane_s >= h * HEAD_DIM) & (lane_s < (h + 1) * HEAD_DIM)
        spare = (1 - h) * HEAD_DIM
        ind = jnp.where(lane_s == spare + blk, 1.0, 0.0)
        kaug_ref[h] = jnp.where(own_s, kf, ind).astype(BF16)
        own8 = (lane8 >= h * HEAD_DIM) & (lane8 < (h + 1) * HEAD_DIM)
        km.append(_split_bf16(jnp.where(own8, kmean, 0.0)))

    lane = lax.broadcasted_iota(I32, (BQ, LANES), 1)
    jidx = lax.broadcasted_iota(I32, (nb, BQ), 0)
    causal = lax.broadcasted_iota(I32, (BQ, BQ), 1) <= lax.broadcasted_iota(I32, (BQ, BQ), 0)

    for qb in range(nb):
        rows = slice(qb * BQ, (qb + 1) * BQ)
        q16 = q_ref[0, rows, :]
        qf = q16.astype(F32)
        outs = []
        for h in range(2):
            own = (lane >= h * HEAD_DIM) & (lane < (h + 1) * HEAD_DIM)
            spare = (1 - h) * HEAD_DIM
            qs = jnp.where(own, qf * scale, 0.0)
            if qb > MOBA_TOPK:
                gate = _dot_nt(km[h][0], q16) + _dot_nt(km[h][1], q16)
                gm = jnp.where(jidx < qb, gate, -jnp.inf)
                rank = jnp.zeros((nb, BQ), F32)
                for j2 in range(qb):
                    other = gm[j2:j2 + 1, :]
                    beats = (other > gm) | ((other == gm) & (j2 < jidx))
                    rank = rank + jnp.where(beats, 1.0, 0.0)
                keep = (rank < float(MOBA_TOPK)) | (jidx >= qb)
                pen_t = jnp.where(keep, 0.0, MASK_PENALTY)
                pieces = [pen_t, jnp.zeros((LANES - spare - nb, BQ), F32)]
                if spare > 0:
                    pieces.insert(0, jnp.zeros((spare, BQ), F32))
                q_aug = (qs + jnp.concatenate(pieces, axis=0).T).astype(BF16)
            else:
                q_aug = qs.astype(BF16)

            mt = None
            for j in range(qb + 1):
                cols = slice(j * BQ, (j + 1) * BQ)
                s = _dot_nt(q_aug, kaug_ref[h, cols, :])
                if j == qb:
                    s = jnp.where(causal, s, -jnp.inf)
                s_ref[h, :, cols] = s
                t = jnp.maximum(s[:, :LANES], s[:, LANES:])
                mt = t if mt is None else jnp.maximum(mt, t)
            m = jnp.max(mt, axis=-1, keepdims=True)
            lt = None
            acc = None
            for j in range(qb + 1):
                cols = slice(j * BQ, (j + 1) * BQ)
                p = jnp.exp(s_ref[h, :, cols] - m)
                pt = p[:, :LANES] + p[:, LANES:]
                lt = pt if lt is None else lt + pt
                pv = _dot(p.astype(BF16), v_ref[0, cols, :])
                acc = pv if acc is None else acc + pv
            outs.append(acc / jnp.sum(lt, axis=-1, keepdims=True))
        o_ref[0, rows, :] = jnp.where(lane < HEAD_DIM, outs[0], outs[1])


def _moba_attention(qkv3):
    B, S, _ = qkv3.shape
    n_pairs = ATTN_WIDTH // LANES
    return pl.pallas_call(
        _moba_kernel,
        grid=(B, n_pairs),
        in_specs=[
            pl.BlockSpec((1, S, LANES), lambda b, p: (b, 0, p)),
            pl.BlockSpec((1, S, LANES), lambda b, p: (b, 0, n_pairs + p)),
            pl.BlockSpec((1, S, LANES), lambda b, p: (b, 0, 2 * n_pairs + p)),
        ],
        out_specs=pl.BlockSpec((1, S, LANES), lambda b, p: (b, 0, p)),
        out_shape=jax.ShapeDtypeStruct((B, S, ATTN_WIDTH), F32),
        scratch_shapes=[pltpu.VMEM((2, S, LANES), BF16), pltpu.VMEM((2, MOBA_BLOCK, S), F32)],
        compiler_params=_params("parallel", "parallel"),
        name="moba_attention",
    )(qkv3, qkv3, qkv3)


def _conv_kernel(x_ref, wa32_ref, wg32_ref, cw_ref, cb_ref, g_ref, b_ref, o_ref, ubuf, w_ref, ush):
    si = pl.program_id(1)
    tm = x_ref.shape[1]

    @pl.when((pl.program_id(0) == 0) & (si == 0))
    def _cast_weights():
        w_ref[:, :CONV_WIDTH] = wa32_ref[...].astype(BF16)
        w_ref[:, CONV_WIDTH:] = wg32_ref[...].astype(BF16)

    @pl.when(si == 0)
    def _zero_history():
        ubuf[0:CONV_HALO, :] = jnp.zeros((CONV_HALO, CONV_WIDTH), F32)

    @pl.when(si > 0)
    def _carry_history():
        ubuf[0:CONV_HALO, :] = ubuf[tm:tm + CONV_HALO, :]

    u2 = _dot(x_ref[0].astype(BF16), w_ref[...])
    ubuf[CONV_HALO:CONV_HALO + tm, :] = u2[:, :CONV_WIDTH] * jax.nn.sigmoid(u2[:, CONV_WIDTH:])
    n_sh = ush.shape[1]
    for s in range(1, SUBLANES):
        ush[s - 1] = ubuf[s:s + n_sh, :]
    base = CONV_HALO - (CONV_KERNEL - 1)
    for rc in range(tm // CONV_ROWS):
        parts = []
        for c in range(CONV_WIDTH // LANES):
            lanes = slice(c * LANES, (c + 1) * LANES)
            acc = jnp.broadcast_to(cb_ref[:, lanes], (CONV_ROWS, LANES))
            for j in range(CONV_KERNEL):
                s = (base + j) % SUBLANES
                r0 = base + j - s + rc * CONV_ROWS
                src = ubuf[r0:r0 + CONV_ROWS, lanes] if s == 0 else ush[s - 1, r0:r0 + CONV_ROWS, lanes]
                acc = acc + cw_ref[j:j + 1, lanes] * src
            parts.append(acc)
        y = _layer_norm(jnp.concatenate(parts, axis=1), g_ref[...], b_ref[...])
        o_ref[0, rc * CONV_ROWS:(rc + 1) * CONV_ROWS, :] = (y * jax.nn.sigmoid(y)).astype(BF16)


def _conv_branch(x3, w_in, layer, cw, cb, g, b, tm):
    B, S, D = x3.shape
    first = 3 * ATTN_WIDTH // CONV_WIDTH
    return pl.pallas_call(
        _conv_kernel,
        grid=(B, S // tm),
        in_specs=[
            pl.BlockSpec((1, tm, D), lambda bb, i: (bb, i, 0)),
            pl.BlockSpec((None, D, CONV_WIDTH), lambda bb, i: (layer, 0, first)),
            pl.BlockSpec((None, D, CONV_WIDTH), lambda bb, i: (layer, 0, first + 1)),
            pl.BlockSpec((CONV_KERNEL, CONV_WIDTH), lambda bb, i: (0, 0)),
            pl.BlockSpec((1, CONV_WIDTH), lambda bb, i: (0, 0)),
            pl.BlockSpec((1, CONV_WIDTH), lambda bb, i: (0, 0)),
            pl.BlockSpec((1, CONV_WIDTH), lambda bb, i: (0, 0)),
        ],
        out_specs=pl.BlockSpec((1, tm, CONV_WIDTH), lambda bb, i: (bb, i, 0)),
        out_shape=jax.ShapeDtypeStruct((B, S, CONV_WIDTH), BF16),
        scratch_shapes=[
            pltpu.VMEM((CONV_HALO + tm, CONV_WIDTH), F32),
            pltpu.VMEM((D, 2 * CONV_WIDTH), BF16),
            pltpu.VMEM((SUBLANES - 1, CONV_HALO + tm - SUBLANES, CONV_WIDTH), F32),
        ],
        compiler_params=_params("arbitrary", "arbitrary"),
        name="conv_branch",
    )(x3, w_in, w_in, cw, cb, g, b)


def _mix_kernel(ya_ref, yc_ref, x_ref, gain_ref, wo32_ref, g_ref, b_ref, wrh_ref, wrl_ref, br_ref,
                x1_ref, idx_ref, gate_ref, wo_ref):
    tm = x_ref.shape[0]

    @pl.when(pl.program_id(0) == 0)
    def _cast_weights():
        wo_ref[...] = wo32_ref[...].astype(BF16)

    ya = ya_ref[...]
    ms = jnp.mean(ya * ya, axis=-1, keepdims=True)
    yan = ya * lax.rsqrt(ms + LN_EPS) * gain_ref[...]
    mix = _dot(yan.astype(BF16), wo_ref[0:ATTN_WIDTH, :]) + _dot(yc_ref[...], wo_ref[ATTN_WIDTH:, :])
    x1 = _layer_norm(DEEPNORM_ALPHA * x_ref[...] + mix, g_ref[...], b_ref[...])
    x1_ref[...] = x1
    xh, xl = _split_bf16(x1)
    wrh = wrh_ref[...]
    logit = _dot_nt(wrh, xh) + _dot_nt(wrl_ref[...], xh) + _dot_nt(wrh, xl) + br_ref[...]
    eidx = lax.broadcasted_iota(I32, (N_EXPERTS, tm), 0)
    vals, idxs = [], []
    for _ in range(TOP_K):
        m = jnp.max(logit, axis=0, keepdims=True)
        idx = jnp.min(jnp.where(logit == m, eidx, N_EXPERTS), axis=0, keepdims=True)
        vals.append(m)
        idxs.append(idx)
        logit = jnp.where(eidx == idx, -jnp.inf, logit)
    es = [jnp.exp(v - vals[0]) for v in vals]
    tot = es[0] + es[1] + es[2] + es[3]
    idx_ref[...] = jnp.concatenate(idxs, axis=0)
    gate_ref[...] = jnp.concatenate([e / tot for e in es], axis=0)


def _mix_router(ya, yc, x2d, gain, w_o, layer, g, b, wr_hi, wr_lo, b_r, tm):
    T, D = x2d.shape
    full = lambda shape: pl.BlockSpec(shape, lambda i: (0, 0))
    return pl.pallas_call(
        _mix_kernel,
        grid=(T // tm,),
        in_specs=[
            pl.BlockSpec((tm, ATTN_WIDTH), lambda i: (i, 0)),
            pl.BlockSpec((tm, CONV_WIDTH), lambda i: (i, 0)),
            pl.BlockSpec((tm, D), lambda i: (i, 0)),
            full((1, ATTN_WIDTH)),
            pl.BlockSpec((None, D, D), lambda i: (layer, 0, 0)),
            full((1, D)),
            full((1, D)),
            full((N_EXPERTS, D)),
            full((N_EXPERTS, D)),
            full((N_EXPERTS, 1)),
        ],
        out_specs=[
            pl.BlockSpec((tm, D), lambda i: (i, 0)),
            pl.BlockSpec((TOP_K, tm), lambda i: (0, i)),
            pl.BlockSpec((TOP_K, tm), lambda i: (0, i)),
        ],
        out_shape=[
            jax.ShapeDtypeStruct((T, D), F32),
            jax.ShapeDtypeStruct((TOP_K, T), I32),
            jax.ShapeDtypeStruct((TOP_K, T), F32),
        ],
        scratch_shapes=[pltpu.VMEM((D, D), BF16)],
        compiler_params=_params("arbitrary"),
        name="mix_router",
    )(ya, yc, x2d, gain, w_o, g, b, wr_hi, wr_lo, b_r)


def _rank_kernel(idx_ref, tri_ref, low_ref, pos_ref, cnt_ref, before_ref, carry):
    tt = idx_ref.shape[1]

    @pl.when(pl.program_id(0) == 0)
    def _init():
        carry[...] = jnp.zeros_like(carry)

    eidx = lax.broadcasted_iota(I32, (N_EXPERTS, tt), 0)
    idx = idx_ref[...]
    onehots = [eidx == idx[k:k + 1, :] for k in range(TOP_K)]
    sel = jnp.zeros((N_EXPERTS, tt), F32)
    for oh in onehots:
        sel = sel + jnp.where(oh, 1.0, 0.0)
    prefix = _dot(sel.astype(BF16), tri_ref[...])
    cnt = jnp.broadcast_to(jnp.sum(sel, axis=1, keepdims=True), (N_EXPERTS, LANES))
    start = _dot(low_ref[...], cnt.astype(BF16))
    local = prefix + start[:, 0:1]
    pos = [jnp.sum(jnp.where(oh, local, 0.0), axis=0, keepdims=True) for oh in onehots]
    pos_ref[...] = jnp.concatenate(pos, axis=0).astype(I32)
    cnt_ref[...] = cnt
    before_ref[...] = carry[...]
    carry[...] = carry[...] + cnt


def _tile_sort_positions(top_idx_t, tt):
    T = top_idx_t.shape[1]
    assert tt <= 256, "per-tile expert counts must stay exactly representable in bf16"
    tri = (jnp.arange(tt)[:, None] < jnp.arange(tt)[None, :]).astype(BF16)
    low = (jnp.arange(N_EXPERTS)[None, :] < jnp.arange(N_EXPERTS)[:, None]).astype(BF16)
    per_tile = pl.BlockSpec((None, N_EXPERTS, LANES), lambda i: (i, 0, 0))
    return pl.pallas_call(
        _rank_kernel,
        grid=(T // tt,),
        in_specs=[
            pl.BlockSpec((TOP_K, tt), lambda i: (0, i)),
            pl.BlockSpec((tt, tt), lambda i: (0, 0)),
            pl.BlockSpec((N_EXPERTS, N_EXPERTS), lambda i: (0, 0)),
        ],
        out_specs=[pl.BlockSpec((TOP_K, tt), lambda i: (0, i)), per_tile, per_tile],
        out_shape=[
            jax.ShapeDtypeStruct((TOP_K, T), I32),
            jax.ShapeDtypeStruct((T // tt, N_EXPERTS, LANES), F32),
            jax.ShapeDtypeStruct((T // tt, N_EXPERTS, LANES), F32),
        ],
        scratch_shapes=[pltpu.VMEM((N_EXPERTS, LANES), F32)],
        compiler_params=_params("arbitrary"),
        name="tile_sort_positions",
    )(top_idx_t, tri, low)


SEGMENT_CHUNKS = (256, 128, 64, 32, 16, 8, 4, 2, 1)
SEGMENT_RARE = 64


def _start_segment_copies(n_rows, make_copy):
    def arms(chunks):
        for p in chunks:
            offset = n_rows & ~(2 * p - 1)

            @pl.when((n_rows & p) != 0)
            def _():
                make_copy(offset, p).start()

    arms([p for p in SEGMENT_CHUNKS if p < SEGMENT_RARE])

    @pl.when(n_rows >= SEGMENT_RARE)
    def _():
        arms([p for p in SEGMENT_CHUNKS if p >= SEGMENT_RARE])


ROW_TILES = 8


def _token_rows(start, n):
    return pl.ds(pl.multiple_of(start * ROW_TILES, ROW_TILES), n * ROW_TILES)


def _store_token_major(ref, val):
    n = val.shape[0]
    for c in range(ROW_TILES):
        ref[pl.ds(c, n, stride=ROW_TILES), :] = val[:, c * LANES:(c + 1) * LANES]


def _load_token_major(ref, n):
    return jnp.concatenate([ref[pl.ds(c, n, stride=ROW_TILES), :] for c in range(ROW_TILES)], axis=1)


def _dispatch_kernel(zrow_ref, src_ref, len_ref, dst_ref, pos_ref, x_ref, xs_ref, sorted_ref, zbuf, zsem, sem):
    i = pl.program_id(0)
    n_tiles = pl.num_programs(0)
    tt = x_ref.shape[0]
    A = TOP_K * tt
    slot = i % 2

    def zero_copy(e):
        r = pl.multiple_of(jnp.maximum(zrow_ref[e], 0), MOE_SUPER)
        return pltpu.make_async_copy(zbuf, xs_ref.at[_token_rows(r, MOE_SUPER), :], zsem)

    @pl.when(pl.program_id(0) == 0)
    def _zero_padding_rows():
        zbuf[...] = jnp.zeros_like(zbuf)
        for e in range(2 * N_EXPERTS):
            @pl.when(zrow_ref[e] >= 0)
            def _():
                zero_copy(e).start()
        for e in range(2 * N_EXPERTS):
            @pl.when(zrow_ref[e] >= 0)
            def _():
                zero_copy(e).wait()

    def drain(s):
        pltpu.make_async_copy(sorted_ref.at[s], xs_ref.at[_token_rows(0, A), :], sem.at[s]).wait()

    @pl.when(i >= 2)
    def _slot_is_free():
        drain(slot)

    r = lax.broadcasted_iota(I32, (A, tt), 0)
    pos = pos_ref[...]
    onehot = jnp.zeros((A, tt), F32)
    for k in range(TOP_K):
        onehot = jnp.where(r == pos[k:k + 1, :], 1.0, onehot)
    _store_token_major(sorted_ref.at[slot], _dot(onehot.astype(BF16), x_ref[...].astype(BF16)))

    for e in range(N_EXPERTS):
        seg = i * N_EXPERTS + e
        src, dst = src_ref[seg], dst_ref[seg]
        _start_segment_copies(len_ref[seg], lambda off, p: pltpu.make_async_copy(
            sorted_ref.at[slot, _token_rows(src + off, p), :], xs_ref.at[_token_rows(dst + off, p), :],
            sem.at[slot]))

    @pl.when(i == n_tiles - 1)
    def _finish():
        drain(slot)

        @pl.when(i >= 1)
        def _():
            drain(1 - slot)


def _dispatch(zrow, seg_src, seg_len, seg_dst, pos_t, x1, n_rows, tt):
    T, D = x1.shape
    return pl.pallas_call(
        _dispatch_kernel,
        grid_spec=pltpu.PrefetchScalarGridSpec(
            num_scalar_prefetch=4,
            grid=(T // tt,),
            in_specs=[
                pl.BlockSpec((TOP_K, tt), lambda i, *_: (0, i)),
                pl.BlockSpec((tt, D), lambda i, *_: (i, 0)),
            ],
            out_specs=pl.BlockSpec(memory_space=pl.ANY),
            scratch_shapes=[
                pltpu.VMEM((2, TOP_K * tt * ROW_TILES, LANES), F32),
                pltpu.VMEM((MOE_SUPER * ROW_TILES, LANES), F32),
                pltpu.SemaphoreType.DMA(()),
                pltpu.SemaphoreType.DMA((2,)),
            ],
        ),
        out_shape=jax.ShapeDtypeStruct((n_rows * ROW_TILES, LANES), F32),
        compiler_params=_params("arbitrary"),
        name="dispatch",
    )(zrow, seg_src, seg_len, seg_dst, pos_t, x1)


def _moe_kernel(be_ref, nsub_ref, first_ref, nxt_ref, nu_ref, xs_ref, wgu_hbm, bgu_ref, wdn_hbm, bdn_ref, y_ref,
                wgu32, wdn32, wgu_ref, wdn_ref, slot_ref, sem, *, layer):
    d_ff = wdn_ref.shape[0]
    g = pl.program_id(0)

    def weight_copies(e, slot):
        return (pltpu.make_async_copy(wgu_hbm.at[layer, e], wgu32.at[slot], sem.at[0, slot]),
                pltpu.make_async_copy(wdn_hbm.at[layer, e], wdn32.at[slot], sem.at[1, slot]))

    @pl.when(g == 0)
    def _fetch_first_expert():
        slot_ref[0] = 0
        for cp in weight_copies(be_ref[0], 0):
            cp.start()

    @pl.when(first_ref[g] == 1)
    def _switch_expert():
        slot = slot_ref[0]
        for cp in weight_copies(be_ref[g], slot):
            cp.wait()
        wgu_ref[...] = wgu32[slot].astype(BF16)
        wdn_ref[...] = wdn32[slot].astype(BF16)

        @pl.when(nxt_ref[g] >= 0)
        def _fetch_next_expert():
            for cp in weight_copies(nxt_ref[g], 1 - slot):
                cp.start()

        slot_ref[0] = 1 - slot

    for sb in range(MOE_SUB_BLOCKS):
        rows = pl.ds(sb * MOE_BLOCK * ROW_TILES, MOE_BLOCK * ROW_TILES)

        @pl.when(sb < nsub_ref[g])
        def _compute():
            x = _load_token_major(xs_ref.at[rows, :], MOE_BLOCK).astype(BF16)
            h = _dot(x, wgu_ref[...]) + bgu_ref[0]
            x_glu = jnp.minimum(h[:, :d_ff], SWIGLU_LIMIT)
            x_lin = jnp.clip(h[:, d_ff:], -SWIGLU_LIMIT, SWIGLU_LIMIT)
            act = x_glu * jax.nn.sigmoid(SWIGLU_ALPHA * x_glu) * (x_lin + 1.0)
            _store_token_major(y_ref.at[rows, :], _dot(act.astype(BF16), wdn_ref[...]) + bdn_ref[0])

        @pl.when(sb >= nsub_ref[g])
        def _no_rows():
            y_ref[rows, :] = jnp.zeros((MOE_BLOCK * ROW_TILES, LANES), F32)


def _moe_blocks(block_e, n_sub, first, nxt, n_used, xs, w_gu, b_gu, w_dn, b_dn, layer):
    D = w_gu.shape[2]
    assert D == ROW_TILES * LANES
    n_super = xs.shape[0] // (MOE_SUPER * ROW_TILES)
    d_ff = w_dn.shape[2]
    exp3 = lambda g, be, ns, fi, nx, nu: (be[g], 0, 0)
    return pl.pallas_call(
        functools.partial(_moe_kernel, layer=layer),
        grid_spec=pltpu.PrefetchScalarGridSpec(
            num_scalar_prefetch=5,
            grid=(n_super,),
            in_specs=[
                pl.BlockSpec((MOE_SUPER * ROW_TILES, LANES),
                             lambda g, be, ns, fi, nx, nu: (jnp.minimum(g, nu[0] - 1), 0)),
                pl.BlockSpec(memory_space=pl.ANY),
                pl.BlockSpec((1, 1, 2 * d_ff), exp3),
                pl.BlockSpec(memory_space=pl.ANY),
                pl.BlockSpec((1, 1, D), exp3),
            ],
            out_specs=pl.BlockSpec((MOE_SUPER * ROW_TILES, LANES), lambda g, *_: (g, 0)),
            scratch_shapes=[
                pltpu.VMEM((2, D, 2 * d_ff), F32),
                pltpu.VMEM((2, d_ff, D), F32),
                pltpu.VMEM((D, 2 * d_ff), BF16),
                pltpu.VMEM((d_ff, D), BF16),
                pltpu.SMEM((1,), I32),
                pltpu.SemaphoreType.DMA((2, 2)),
            ],
        ),
        out_shape=jax.ShapeDtypeStruct(xs.shape, F32),
        compiler_params=_params("arbitrary"),
        name="moe_blocks",
    )(block_e, n_sub, first, nxt, n_used, xs, w_gu, b_gu, w_dn, b_dn)


def _combine_kernel(src_ref, len_ref, dst_ref, pos_ref, gate_ref, x1_ref, g_ref, b_ref, y_hbm, o_ref, ybuf, sem):
    i = pl.program_id(0)
    n_tiles = pl.num_programs(0)
    tt = x1_ref.shape[0]
    A = TOP_K * tt
    slot = i % 2

    def fetch(tile, s):
        for e in range(N_EXPERTS):
            seg = tile * N_EXPERTS + e
            src, dst = src_ref[seg], dst_ref[seg]
            _start_segment_copies(len_ref[seg], lambda off, p: pltpu.make_async_copy(
                y_hbm.at[_token_rows(dst + off, p), :], ybuf.at[s, _token_rows(src + off, p), :], sem.at[s]))

    @pl.when(i == 0)
    def _first_tile():
        fetch(0, 0)

    @pl.when(i + 1 < n_tiles)
    def _prefetch_next_tile():
        fetch(i + 1, 1 - slot)

    pltpu.make_async_copy(y_hbm.at[_token_rows(0, A), :], ybuf.at[slot], sem.at[slot]).wait()

    c = lax.broadcasted_iota(I32, (tt, A), 1)
    pos = pos_ref[...]
    gates = gate_ref[...]
    weights = jnp.zeros((tt, A), F32)
    for k in range(TOP_K):
        weights = jnp.where(c == pos[:, k:k + 1], gates[:, k:k + 1], weights)
    w_hi, w_lo = _split_bf16(weights)
    yl = _load_token_major(ybuf.at[slot], A).astype(BF16)
    y = _dot(w_hi, yl) + _dot(w_lo, yl)
    o_ref[...] = _layer_norm(DEEPNORM_ALPHA * x1_ref[...] + y, g_ref[...], b_ref[...])


def _combine(seg_src, seg_len, seg_dst, pos, gates, x1, g, b, y_buf, tt):
    T, D = x1.shape
    return pl.pallas_call(
        _combine_kernel,
        grid_spec=pltpu.PrefetchScalarGridSpec(
            num_scalar_prefetch=3,
            grid=(T // tt,),
            in_specs=[
                pl.BlockSpec((tt, TOP_K), lambda i, *_: (i, 0)),
                pl.BlockSpec((tt, TOP_K), lambda i, *_: (i, 0)),
                pl.BlockSpec((tt, D), lambda i, *_: (i, 0)),
                pl.BlockSpec((1, D), lambda i, *_: (0, 0)),
                pl.BlockSpec((1, D), lambda i, *_: (0, 0)),
                pl.BlockSpec(memory_space=pl.ANY),
            ],
            out_specs=pl.BlockSpec((tt, D), lambda i, *_: (i, 0)),
            scratch_shapes=[pltpu.VMEM((2, TOP_K * tt * ROW_TILES, LANES), F32), pltpu.SemaphoreType.DMA((2,))],
        ),
        out_shape=jax.ShapeDtypeStruct((T, D), F32),
        compiler_params=_params("arbitrary"),
        name="combine",
    )(seg_src, seg_len, seg_dst, pos, gates, x1, g, b, y_buf)


def _rotary_lane_tables(positions):
    half = ROT_DIM // 2
    inv_freq = ROPE_THETA ** (-jnp.arange(0, ROT_DIM, 2, dtype=F32) / ROT_DIM)
    ang = positions.reshape(-1).astype(F32)[:, None] * inv_freq
    cos, sin = jnp.cos(ang), jnp.sin(ang)
    r = jnp.arange(LANES) % HEAD_DIM
    first = (r < half)[None, :]
    second = ((r >= half) & (r < ROT_DIM))[None, :]
    cos_l = jnp.take(cos, r % half, axis=1)
    sin_l = jnp.take(sin, r % half, axis=1)
    c = jnp.where(first | second, cos_l, 1.0)
    s1 = jnp.where(first, -sin_l, 0.0)
    s2 = jnp.where(second, sin_l, 0.0)
    return c, s1, s2


def _route_metadata(tile_cnt, tile_before, n_super):
    tile_cnt = tile_cnt.astype(I32)
    tile_before = tile_before.astype(I32)
    experts = jnp.arange(N_EXPERTS, dtype=I32)
    counts = tile_before[-1] + tile_cnt[-1]
    padded = ((counts + MOE_SUPER - 1) // MOE_SUPER) * MOE_SUPER
    pad_end = jnp.cumsum(padded)
    pad_start = pad_end - padded
    n_used = pad_end[-1] // MOE_SUPER
    g = jnp.arange(n_super, dtype=I32)
    used = g < n_used
    g_row = jnp.minimum(g, n_used - 1) * MOE_SUPER
    block_e = jnp.clip(jnp.sum((pad_end[None, :] <= g_row[:, None]).astype(I32), axis=1), 0, N_EXPERTS - 1)
    of_block = lambda table: jnp.sum(jnp.where(block_e[:, None] == experts[None, :], table[None, :], 0), axis=1)
    rows_in = jnp.clip(of_block(counts) - (g_row - of_block(pad_start)), 0, MOE_SUPER)
    n_sub = jnp.where(used, (rows_in + MOE_BLOCK - 1) // MOE_BLOCK, 0)
    first = (used & (g_row == of_block(pad_start))).astype(I32)
    later_nonempty = (experts[None, :] > experts[:, None]) & (counts[None, :] > 0)
    nxt_e = jnp.min(jnp.where(later_nonempty, experts[None, :], N_EXPERTS), axis=1)
    nxt = of_block(jnp.where(nxt_e == N_EXPERTS, -1, nxt_e))
    tail = n_used + experts
    zrow = jnp.concatenate([jnp.where(padded > 0, pad_end - MOE_SUPER, -1),
                            jnp.where(tail < n_super, tail * MOE_SUPER, -1)]).astype(I32)
    seg_src = (jnp.cumsum(tile_cnt, axis=1) - tile_cnt).reshape(-1)
    seg_dst = (pad_start[None, :] + tile_before).reshape(-1)
    moe_scalars = (block_e, n_sub.astype(I32), first, nxt.astype(I32), n_used.reshape(1).astype(I32))
    return moe_scalars, zrow, seg_src, tile_cnt.reshape(-1), seg_dst


def kernel(x, positions, w_in, attn_gain, conv_w, conv_b, conv_ln_g, conv_ln_b, w_o, ln1_g, ln1_b,
           w_router, b_router, w_gu, b_gu, w_dn, b_dn, ln2_g, ln2_b):
    B, S, D = x.shape
    T = B * S
    depth = w_in.shape[0]
    tm = 512
    tt = 256
    n_super = T * TOP_K // MOE_SUPER + N_EXPERTS
    n_rows = n_super * MOE_SUPER

    rot_c, rot_s1, rot_s2 = _rotary_lane_tables(positions)
    row = lambda a: a.reshape(1, -1)

    x2d = x.reshape(T, D)
    for l in range(depth):
        qkv = _qkv_proj(x2d, w_in, l, rot_c, rot_s1, rot_s2, tm)
        ya = _moba_attention(qkv.reshape(B, S, 3 * ATTN_WIDTH)).reshape(T, ATTN_WIDTH)
        yc = _conv_branch(x2d.reshape(B, S, D), w_in, l, conv_w[l], row(conv_b[l]), row(conv_ln_g[l]),
                          row(conv_ln_b[l]), tm).reshape(T, CONV_WIDTH)
        wr_hi, wr_lo = _split_bf16(w_router[l].T)
        x1, top_idx_t, gates_t = _mix_router(
            ya, yc, x2d, row(attn_gain[l]), w_o, l, row(ln1_g[l]), row(ln1_b[l]),
            wr_hi, wr_lo, b_router[l].reshape(N_EXPERTS, 1), tm)
        pos_t, tile_cnt, tile_before = _tile_sort_positions(top_idx_t, tt)
        moe_scalars, zrow, seg_src, seg_len, seg_dst = _route_metadata(
            tile_cnt[:, :, 0], tile_before[:, :, 0], n_super)
        xs = _dispatch(zrow, seg_src, seg_len, seg_dst, pos_t, x1, n_rows, tt)
        y_buf = _moe_blocks(*moe_scalars, xs, w_gu, b_gu[l][:, None, :], w_dn, b_dn[l][:, None, :], l)
        x2d = _combine(seg_src, seg_len, seg_dst, pos_t.T, gates_t.T, x1, row(ln2_g[l]), row(ln2_b[l]),
                       y_buf, tt)
    return x2d.reshape(B, S, D)
```

```python
import functools

import jax
import jax.numpy as jnp
from jax import lax
from jax.experimental import pallas as pl
from jax.experimental.pallas import tpu as pltpu

F32 = jnp.float32
BF16 = jnp.bfloat16
I32 = jnp.int32

HEAD_DIM = 64
ATTN_HEADS = 8
ATTN_WIDTH = ATTN_HEADS * HEAD_DIM
CONV_WIDTH = 512
ROT_DIM = HEAD_DIM // 4
ROPE_THETA = 500000.0
MOBA_BLOCK = 256
MOBA_TOPK = 3
CONV_KERNEL = 31
N_EXPERTS = 32
TOP_K = 4
MOE_BLOCK = 256
MOE_SUB_BLOCKS = 2
MOE_SUPER = MOE_BLOCK * MOE_SUB_BLOCKS
SWIGLU_LIMIT = 7.0
SWIGLU_ALPHA = 1.702
DEEPNORM_ALPHA = (2.0 * 4) ** 0.25
LN_EPS = 1e-5

LANES = 128
SUBLANES = 8
CONV_ROWS = 64
CONV_HALO = 32
MASK_PENALTY = -1e30
LOG2_E = 1.4426950408889634
MOBA_LOOKAHEAD = 1
VMEM_LIMIT = 48 * 1024 * 1024

_NT = (((1,), (1,)), ((), ()))


def _params(*sem):
    return pltpu.CompilerParams(dimension_semantics=sem, vmem_limit_bytes=VMEM_LIMIT)


def _dot(a, b):
    return jnp.dot(a, b, preferred_element_type=F32)


def _dot_nt(a, b):
    return lax.dot_general(a, b, _NT, preferred_element_type=F32)


def _split_bf16(a):
    hi = a.astype(BF16)
    lo = (a - hi.astype(F32)).astype(BF16)
    return hi, lo


def _layer_norm(z, g, b):
    mu = jnp.mean(z, axis=-1, keepdims=True)
    zc = z - mu
    var = jnp.mean(zc * zc, axis=-1, keepdims=True)
    return zc * lax.rsqrt(var + LN_EPS) * g + b


def _moba_kernel(q_ref, k_ref, v_ref, o_ref, kaug_ref, vaug_ref, s_ref):
    S = k_ref.shape[1]
    nb = S // MOBA_BLOCK
    BQ = MOBA_BLOCK
    scale = HEAD_DIM ** -0.5 * LOG2_E

    kf = k_ref[0].astype(F32)
    vf = v_ref[0].astype(F32)
    kmean = jnp.sum(kf.reshape(nb, MOBA_BLOCK, LANES), axis=1) / float(MOBA_BLOCK)
    lane_s = lax.broadcasted_iota(I32, (S, LANES), 1)
    blk = lax.broadcasted_iota(I32, (S, LANES), 0) // MOBA_BLOCK
    lane8 = lax.broadcasted_iota(I32, (nb, LANES), 1)
    km = []
    for h in range(2):
        own_s = (lane_s >= h * HEAD_DIM) & (lane_s < (h + 1) * HEAD_DIM)
        spare = (1 - h) * HEAD_DIM
        ind = jnp.where(lane_s == spare + blk, 1.0, 0.0)
        kaug_ref[h] = jnp.where(own_s, kf, ind).astype(BF16)
        vaug_ref[h] = jnp.where(own_s, vf, 1.0).astype(BF16)
        own8 = (lane8 >= h * HEAD_DIM) & (lane8 < (h + 1) * HEAD_DIM)
        km.append(_split_bf16(jnp.where(own8, kmean, 0.0)))

    lane = lax.broadcasted_iota(I32, (BQ, LANES), 1)
    jidx = lax.broadcasted_iota(I32, (nb, BQ), 0)
    causal = lax.broadcasted_iota(I32, (BQ, BQ), 1) <= lax.broadcasted_iota(I32, (BQ, BQ), 0)

    def scores(qb, h):
        q16 = q_ref[0, qb * BQ:(qb + 1) * BQ, :]
        own = (lane >= h * HEAD_DIM) & (lane < (h + 1) * HEAD_DIM)
        spare = (1 - h) * HEAD_DIM
        qs = jnp.where(own, q16.astype(F32) * scale, 0.0)
        if qb > MOBA_TOPK:
            gate = _dot_nt(km[h][0], q16) + _dot_nt(km[h][1], q16)
            gm = jnp.where(jidx < qb, gate, -jnp.inf)
            rank = jnp.zeros((nb, BQ), F32)
            for j2 in range(qb):
                other = gm[j2:j2 + 1, :]
                beats = (other > gm) | ((other == gm) & (j2 < jidx))
                rank = rank + jnp.where(beats, 1.0, 0.0)
            keep = (rank < float(MOBA_TOPK)) | (jidx >= qb)
            pen_t = jnp.where(keep, 0.0, MASK_PENALTY)
            pieces = [pen_t, jnp.zeros((LANES - spare - nb, BQ), F32)]
            if spare > 0:
                pieces.insert(0, jnp.zeros((spare, BQ), F32))
            q_aug = (qs + jnp.concatenate(pieces, axis=0).T).astype(BF16)
        else:
            q_aug = qs.astype(BF16)
        slot0 = qb * (qb + 1) // 2
        mt = None
        for j in range(qb + 1):
            s = _dot_nt(q_aug, kaug_ref[h, j * BQ:(j + 1) * BQ, :])
            if j == qb:
                s = jnp.where(causal, s, -jnp.inf)
            s_ref[h, slot0 + j] = s
            t = jnp.maximum(s[:, :LANES], s[:, LANES:])
            mt = t if mt is None else jnp.maximum(mt, t)
            yield
        row_max[(qb, h)] = jnp.max(mt, axis=-1, keepdims=True)

    def weighted_values(qb, h):
        m = row_max.pop((qb, h))
        slot0 = qb * (qb + 1) // 2
        acc = None
        for j in range(qb + 1):
            p = jnp.exp2(s_ref[h, slot0 + j] - m)
            pv = _dot(p.astype(BF16), vaug_ref[h, j * BQ:(j + 1) * BQ, :])
            acc = pv if acc is None else acc + pv
            yield
        spare = (1 - h) * HEAD_DIM
        outs[h] = acc / acc[:, spare:spare + 1]

    def emit(phase):
        for _ in phase:
            pass

    units = [(qb, h) for qb in range(nb) for h in range(2)]
    row_max, outs = {}, {}
    for u in units[:MOBA_LOOKAHEAD]:
        emit(scores(*u))
    for i, (qb, h) in enumerate(units):
        if i + MOBA_LOOKAHEAD < len(units):
            emit(scores(*units[i + MOBA_LOOKAHEAD]))
        emit(weighted_values(qb, h))
        if h == 1:
            o_ref[0, qb * BQ:(qb + 1) * BQ, :] = jnp.where(lane < HEAD_DIM, outs[0], outs[1])


def _moba_attention(qkv3):
    B, S, _ = qkv3.shape
    n_pairs = ATTN_WIDTH // LANES
    nb = S // MOBA_BLOCK
    return pl.pallas_call(
        _moba_kernel,
        grid=(B, n_pairs),
        in_specs=[
            pl.BlockSpec((1, S, LANES), lambda b, p: (b, 0, p)),
            pl.BlockSpec((1, S, LANES), lambda b, p: (b, 0, n_pairs + p)),
            pl.BlockSpec((1, S, LANES), lambda b, p: (b, 0, 2 * n_pairs + p)),
        ],
        out_specs=pl.BlockSpec((1, S, LANES), lambda b, p: (b, 0, p)),
        out_shape=jax.ShapeDtypeStruct((B, S, ATTN_WIDTH), F32),
        scratch_shapes=[
            pltpu.VMEM((2, S, LANES), BF16),
            pltpu.VMEM((2, S, LANES), BF16),
            pltpu.VMEM((2, nb * (nb + 1) // 2, MOBA_BLOCK, MOBA_BLOCK), F32),
        ],
        compiler_params=_params("parallel", "parallel"),
        name="moba_attention",
    )(qkv3, qkv3, qkv3)


def _conv_kernel(x_ref, wq32_ref, wa32_ref, wg32_ref, c_ref, s1_ref, s2_ref, cw_ref, cb_ref, g_ref, b_ref,
                 qkv_ref, o_ref, ubuf, w_ref, ush, wq_ref):
    si = pl.program_id(1)
    tm = x_ref.shape[1]

    @pl.when((pl.program_id(0) == 0) & (si == 0))
    def _cast_weights():
        w_ref[:, :CONV_WIDTH] = wa32_ref[...].astype(BF16)
        w_ref[:, CONV_WIDTH:] = wg32_ref[...].astype(BF16)
        wq_ref[...] = wq32_ref[...].astype(BF16)

    @pl.when(si == 0)
    def _zero_history():
        ubuf[0:CONV_HALO, :] = jnp.zeros((CONV_HALO, CONV_WIDTH), F32)

    @pl.when(si > 0)
    def _carry_history():
        ubuf[0:CONV_HALO, :] = ubuf[tm:tm + CONV_HALO, :]

    xb = x_ref[0].astype(BF16)
    u2 = _dot(xb, w_ref[...])
    ubuf[CONV_HALO:CONV_HALO + tm, :] = u2[:, :CONV_WIDTH] * jax.nn.sigmoid(u2[:, CONV_WIDTH:])
    n_sh = ush.shape[1]
    for s in range(1, SUBLANES):
        ush[s - 1] = ubuf[s:s + n_sh, :]

    reps = ATTN_WIDTH // LANES
    c = jnp.concatenate([c_ref[...]] * reps, axis=1)
    s1 = jnp.concatenate([s1_ref[...]] * reps, axis=1)
    s2 = jnp.concatenate([s2_ref[...]] * reps, axis=1)
    half = ROT_DIM // 2

    def project(part):
        cols = slice(part * ATTN_WIDTH, (part + 1) * ATTN_WIDTH)
        h = _dot(xb, wq_ref[:, cols])
        if part < 2:
            h = h * c + pltpu.roll(h, ATTN_WIDTH - half, 1) * s1 + pltpu.roll(h, half, 1) * s2
        qkv_ref[0, :, cols] = h.astype(BF16)

    base = CONV_HALO - (CONV_KERNEL - 1)

    def conv_rows(rc):
        parts = []
        for cg in range(CONV_WIDTH // LANES):
            lanes = slice(cg * LANES, (cg + 1) * LANES)
            acc = jnp.broadcast_to(cb_ref[:, lanes], (CONV_ROWS, LANES))
            for j in range(CONV_KERNEL):
                s = (base + j) % SUBLANES
                r0 = base + j - s + rc * CONV_ROWS
                src = ubuf[r0:r0 + CONV_ROWS, lanes] if s == 0 else ush[s - 1, r0:r0 + CONV_ROWS, lanes]
                acc = acc + cw_ref[j:j + 1, lanes] * src
            parts.append(acc)
        y = _layer_norm(jnp.concatenate(parts, axis=1), g_ref[...], b_ref[...])
        o_ref[0, rc * CONV_ROWS:(rc + 1) * CONV_ROWS, :] = (y * jax.nn.sigmoid(y)).astype(BF16)

    n_chunks = tm // CONV_ROWS
    project_before = {(part * n_chunks) // 3: part for part in range(3)}
    for rc in range(n_chunks):
        if rc in project_before:
            project(project_before[rc])
        conv_rows(rc)


def _in_proj_conv(x3, w_in, layer, rot_c, rot_s1, rot_s2, cw, cb, g, b, tm):
    B, S, D = x3.shape
    N = 3 * ATTN_WIDTH
    tiles = S // tm
    first = N // CONV_WIDTH
    table = pl.BlockSpec((tm, LANES), lambda bb, i: (bb * tiles + i, 0))
    vec = pl.BlockSpec((1, CONV_WIDTH), lambda bb, i: (0, 0))
    return pl.pallas_call(
        _conv_kernel,
        grid=(B, tiles),
        in_specs=[
            pl.BlockSpec((1, tm, D), lambda bb, i: (bb, i, 0)),
            pl.BlockSpec((None, D, N), lambda bb, i: (layer, 0, 0)),
            pl.BlockSpec((None, D, CONV_WIDTH), lambda bb, i: (layer, 0, first)),
            pl.BlockSpec((None, D, CONV_WIDTH), lambda bb, i: (layer, 0, first + 1)),
            table, table, table,
            pl.BlockSpec((CONV_KERNEL, CONV_WIDTH), lambda bb, i: (0, 0)),
            vec, vec, vec,
        ],
        out_specs=[
            pl.BlockSpec((1, tm, N), lambda bb, i: (bb, i, 0)),
            pl.BlockSpec((1, tm, CONV_WIDTH), lambda bb, i: (bb, i, 0)),
        ],
        out_shape=[
            jax.ShapeDtypeStruct((B, S, N), BF16),
            jax.ShapeDtypeStruct((B, S, CONV_WIDTH), BF16),
        ],
        scratch_shapes=[
            pltpu.VMEM((CONV_HALO + tm, CONV_WIDTH), F32),
            pltpu.VMEM((D, 2 * CONV_WIDTH), BF16),
            pltpu.VMEM((SUBLANES - 1, CONV_HALO + tm - SUBLANES, CONV_WIDTH), F32),
            pltpu.VMEM((D, N), BF16),
        ],
        compiler_params=_params("arbitrary", "arbitrary"),
        name="in_proj_conv",
    )(x3, w_in, w_in, w_in, rot_c, rot_s1, rot_s2, cw, cb, g, b)


def _mix_kernel(ya_ref, yc_ref, x_ref, gain_ref, wo32_ref, g_ref, b_ref, wrh_ref, wrl_ref, br_ref,
                x1_ref, idx_ref, gate_ref, wo_ref):
    tm = x_ref.shape[0]

    @pl.when(pl.program_id(0) == 0)
    def _cast_weights():
        wo_ref[...] = wo32_ref[...].astype(BF16)

    ya = ya_ref[...]
    ms = jnp.mean(ya * ya, axis=-1, keepdims=True)
    yan = ya * lax.rsqrt(ms + LN_EPS) * gain_ref[...]
    mix = _dot(yan.astype(BF16), wo_ref[0:ATTN_WIDTH, :]) + _dot(yc_ref[...], wo_ref[ATTN_WIDTH:, :])
    x1 = _layer_norm(DEEPNORM_ALPHA * x_ref[...] + mix, g_ref[...], b_ref[...])
    x1_ref[...] = x1
    xh, xl = _split_bf16(x1)
    wrh = wrh_ref[...]
    logit = _dot_nt(wrh, xh) + _dot_nt(wrl_ref[...], xh) + _dot_nt(wrh, xl) + br_ref[...]
    eidx = lax.broadcasted_iota(I32, (N_EXPERTS, tm), 0)
    vals, idxs = [], []
    for _ in range(TOP_K):
        m = jnp.max(logit, axis=0, keepdims=True)
        idx = jnp.min(jnp.where(logit == m, eidx, N_EXPERTS), axis=0, keepdims=True)
        vals.append(m)
        idxs.append(idx)
        logit = jnp.where(eidx == idx, -jnp.inf, logit)
    es = [jnp.exp(v - vals[0]) for v in vals]
    tot = es[0] + es[1] + es[2] + es[3]
    idx_ref[...] = jnp.concatenate(idxs, axis=0)
    gate_ref[...] = jnp.concatenate([e / tot for e in es], axis=0)


def _mix_router(ya, yc, x2d, gain, w_o, layer, g, b, wr_hi, wr_lo, b_r, tm):
    T, D = x2d.shape
    full = lambda shape: pl.BlockSpec(shape, lambda i: (0, 0))
    return pl.pallas_call(
        _mix_kernel,
        grid=(T // tm,),
        in_specs=[
            pl.BlockSpec((tm, ATTN_WIDTH), lambda i: (i, 0)),
            pl.BlockSpec((tm, CONV_WIDTH), lambda i: (i, 0)),
            pl.BlockSpec((tm, D), lambda i: (i, 0)),
            full((1, ATTN_WIDTH)),
            pl.BlockSpec((None, D, D), lambda i: (layer, 0, 0)),
            full((1, D)),
            full((1, D)),
            full((N_EXPERTS, D)),
            full((N_EXPERTS, D)),
            full((N_EXPERTS, 1)),
        ],
        out_specs=[
            pl.BlockSpec((tm, D), lambda i: (i, 0)),
            pl.BlockSpec((TOP_K, tm), lambda i: (0, i)),
            pl.BlockSpec((TOP_K, tm), lambda i: (0, i)),
        ],
        out_shape=[
            jax.ShapeDtypeStruct((T, D), F32),
            jax.ShapeDtypeStruct((TOP_K, T), I32),
            jax.ShapeDtypeStruct((TOP_K, T), F32),
        ],
        scratch_shapes=[pltpu.VMEM((D, D), BF16)],
        compiler_params=_params("arbitrary"),
        name="mix_router",
    )(ya, yc, x2d, gain, w_o, g, b, wr_hi, wr_lo, b_r)


def _rank_kernel(idx_ref, tri_ref, low_ref, pos_ref, cnt_ref, before_ref, carry):
    tt = idx_ref.shape[1]

    @pl.when(pl.program_id(0) == 0)
    def _init():
        carry[...] = jnp.zeros_like(carry)

    eidx = lax.broadcasted_iota(I32, (N_EXPERTS, tt), 0)
    idx = idx_ref[...]
    onehots = [eidx == idx[k:k + 1, :] for k in range(TOP_K)]
    sel = jnp.zeros((N_EXPERTS, tt), F32)
    for oh in onehots:
        sel = sel + jnp.where(oh, 1.0, 0.0)
    prefix = _dot(sel.astype(BF16), tri_ref[...])
    cnt = jnp.broadcast_to(jnp.sum(sel, axis=1, keepdims=True), (N_EXPERTS, LANES))
    start = _dot(low_ref[...], cnt.astype(BF16))
    local = prefix + start[:, 0:1]
    pos = [jnp.sum(jnp.where(oh, local, 0.0), axis=0, keepdims=True) for oh in onehots]
    pos_ref[...] = jnp.concatenate(pos, axis=0).astype(I32)
    cnt_ref[...] = cnt
    before_ref[...] = carry[...]
    carry[...] = carry[...] + cnt


def _tile_sort_positions(top_idx_t, tt):
    T = top_idx_t.shape[1]
    assert tt <= 256, "per-tile expert counts must stay exactly representable in bf16"
    tri = (jnp.arange(tt)[:, None] < jnp.arange(tt)[None, :]).astype(BF16)
    low = (jnp.arange(N_EXPERTS)[None, :] < jnp.arange(N_EXPERTS)[:, None]).astype(BF16)
    per_tile = pl.BlockSpec((None, N_EXPERTS, LANES), lambda i: (i, 0, 0))
    return pl.pallas_call(
        _rank_kernel,
        grid=(T // tt,),
        in_specs=[
            pl.BlockSpec((TOP_K, tt), lambda i: (0, i)),
            pl.BlockSpec((tt, tt), lambda i: (0, 0)),
            pl.BlockSpec((N_EXPERTS, N_EXPERTS), lambda i: (0, 0)),
        ],
        out_specs=[pl.BlockSpec((TOP_K, tt), lambda i: (0, i)), per_tile, per_tile],
        out_shape=[
            jax.ShapeDtypeStruct((TOP_K, T), I32),
            jax.ShapeDtypeStruct((T // tt, N_EXPERTS, LANES), F32),
            jax.ShapeDtypeStruct((T // tt, N_EXPERTS, LANES), F32),
        ],
        scratch_shapes=[pltpu.VMEM((N_EXPERTS, LANES), F32)],
        compiler_params=_params("arbitrary"),
        name="tile_sort_positions",
    )(top_idx_t, tri, low)


SEGMENT_CHUNKS = (256, 128, 64, 32, 16, 8, 4, 2, 1)
SEGMENT_RARE = 64


def _start_segment_copies(n_rows, make_copy):
    def arms(chunks):
        for p in chunks:
            offset = n_rows & ~(2 * p - 1)

            @pl.when((n_rows & p) != 0)
            def _():
                make_copy(offset, p).start()

    arms([p for p in SEGMENT_CHUNKS if p < SEGMENT_RARE])

    @pl.when(n_rows >= SEGMENT_RARE)
    def _():
        arms([p for p in SEGMENT_CHUNKS if p >= SEGMENT_RARE])


ROW_TILES = 8


def _token_rows(start, n):
    return pl.ds(pl.multiple_of(start * ROW_TILES, ROW_TILES), n * ROW_TILES)


def _store_token_major(ref, val):
    n = val.shape[0]
    for c in range(ROW_TILES):
        ref[pl.ds(c, n, stride=ROW_TILES), :] = val[:, c * LANES:(c + 1) * LANES]


def _load_token_major(ref, n):
    return jnp.concatenate([ref[pl.ds(c, n, stride=ROW_TILES), :] for c in range(ROW_TILES)], axis=1)


def _dispatch_kernel(zrow_ref, src_ref, len_ref, dst_ref, pos_ref, x_ref, xs_ref, sorted_ref, zbuf, zsem, sem):
    i = pl.program_id(0)
    n_tiles = pl.num_programs(0)
    tt = x_ref.shape[0]
    A = TOP_K * tt
    slot = i % 2

    def zero_copy(e):
        r = pl.multiple_of(jnp.maximum(zrow_ref[e], 0), MOE_SUPER)
        return pltpu.make_async_copy(zbuf, xs_ref.at[_token_rows(r, MOE_SUPER), :], zsem)

    @pl.when(pl.program_id(0) == 0)
    def _zero_padding_rows():
        zbuf[...] = jnp.zeros_like(zbuf)
        for e in range(2 * N_EXPERTS):
            @pl.when(zrow_ref[e] >= 0)
            def _():
                zero_copy(e).start()
        for e in range(2 * N_EXPERTS):
            @pl.when(zrow_ref[e] >= 0)
            def _():
                zero_copy(e).wait()

    def drain(s):
        pltpu.make_async_copy(sorted_ref.at[s], xs_ref.at[_token_rows(0, A), :], sem.at[s]).wait()

    @pl.when(i >= 2)
    def _slot_is_free():
        drain(slot)

    r = lax.broadcasted_iota(I32, (A, tt), 0)
    pos = pos_ref[...]
    onehot = jnp.zeros((A, tt), F32)
    for k in range(TOP_K):
        onehot = jnp.where(r == pos[k:k + 1, :], 1.0, onehot)
    _store_token_major(sorted_ref.at[slot], _dot(onehot.astype(BF16), x_ref[...].astype(BF16)))

    for e in range(N_EXPERTS):
        seg = i * N_EXPERTS + e
        src, dst = src_ref[seg], dst_ref[seg]
        _start_segment_copies(len_ref[seg], lambda off, p: pltpu.make_async_copy(
            sorted_ref.at[slot, _token_rows(src + off, p), :], xs_ref.at[_token_rows(dst + off, p), :],
            sem.at[slot]))

    @pl.when(i == n_tiles - 1)
    def _finish():
        drain(slot)

        @pl.when(i >= 1)
        def _():
            drain(1 - slot)


def _dispatch(zrow, seg_src, seg_len, seg_dst, pos_t, x1, n_rows, tt):
    T, D = x1.shape
    return pl.pallas_call(
        _dispatch_kernel,
        grid_spec=pltpu.PrefetchScalarGridSpec(
            num_scalar_prefetch=4,
            grid=(T // tt,),
            in_specs=[
                pl.BlockSpec((TOP_K, tt), lambda i, *_: (0, i)),
                pl.BlockSpec((tt, D), lambda i, *_: (i, 0)),
            ],
            out_specs=pl.BlockSpec(memory_space=pl.ANY),
            scratch_shapes=[
                pltpu.VMEM((2, TOP_K * tt * ROW_TILES, LANES), F32),
                pltpu.VMEM((MOE_SUPER * ROW_TILES, LANES), F32),
                pltpu.SemaphoreType.DMA(()),
                pltpu.SemaphoreType.DMA((2,)),
            ],
        ),
        out_shape=jax.ShapeDtypeStruct((n_rows * ROW_TILES, LANES), F32),
        compiler_params=_params("arbitrary"),
        name="dispatch",
    )(zrow, seg_src, seg_len, seg_dst, pos_t, x1)


def _moe_kernel(be_ref, nsub_ref, first_ref, nxt_ref, nu_ref, xs_ref, wgu_hbm, bgu_ref, wdn_hbm, bdn_ref, y_ref,
                wgu32, wdn32, wgu_ref, wdn_ref, slot_ref, sem, *, layer):
    d_ff = wdn_ref.shape[0]
    g = pl.program_id(0)

    def weight_copies(e, slot):
        return (pltpu.make_async_copy(wgu_hbm.at[layer, e], wgu32.at[slot], sem.at[0, slot]),
                pltpu.make_async_copy(wdn_hbm.at[layer, e], wdn32.at[slot], sem.at[1, slot]))

    @pl.when(g == 0)
    def _fetch_first_expert():
        slot_ref[0] = 0
        for cp in weight_copies(be_ref[0], 0):
            cp.start()

    @pl.when(first_ref[g] == 1)
    def _switch_expert():
        slot = slot_ref[0]
        for cp in weight_copies(be_ref[g], slot):
            cp.wait()
        wgu_ref[...] = wgu32[slot].astype(BF16)
        wdn_ref[...] = wdn32[slot].astype(BF16)

        @pl.when(nxt_ref[g] >= 0)
        def _fetch_next_expert():
            for cp in weight_copies(nxt_ref[g], 1 - slot):
                cp.start()

        slot_ref[0] = 1 - slot

    for sb in range(MOE_SUB_BLOCKS):
        rows = pl.ds(sb * MOE_BLOCK * ROW_TILES, MOE_BLOCK * ROW_TILES)

        @pl.when(sb < nsub_ref[g])
        def _compute():
            x = _load_token_major(xs_ref.at[rows, :], MOE_BLOCK).astype(BF16)
            h = _dot(x, wgu_ref[...]) + bgu_ref[0]
            x_glu = jnp.minimum(h[:, :d_ff], SWIGLU_LIMIT)
            x_lin = jnp.clip(h[:, d_ff:], -SWIGLU_LIMIT, SWIGLU_LIMIT)
            act = x_glu * jax.nn.sigmoid(SWIGLU_ALPHA * x_glu) * (x_lin + 1.0)
            _store_token_major(y_ref.at[rows, :], _dot(act.astype(BF16), wdn_ref[...]) + bdn_ref[0])

        @pl.when(sb >= nsub_ref[g])
        def _no_rows():
            y_ref[rows, :] = jnp.zeros((MOE_BLOCK * ROW_TILES, LANES), F32)


def _moe_blocks(block_e, n_sub, first, nxt, n_used, xs, w_gu, b_gu, w_dn, b_dn, layer):
    D = w_gu.shape[2]
    assert D == ROW_TILES * LANES
    n_super = xs.shape[0] // (MOE_SUPER * ROW_TILES)
    d_ff = w_dn.shape[2]
    exp3 = lambda g, be, ns, fi, nx, nu: (be[g], 0, 0)
    return pl.pallas_call(
        functools.partial(_moe_kernel, layer=layer),
        grid_spec=pltpu.PrefetchScalarGridSpec(
            num_scalar_prefetch=5,
            grid=(n_super,),
            in_specs=[
                pl.BlockSpec((MOE_SUPER * ROW_TILES, LANES),
                             lambda g, be, ns, fi, nx, nu: (jnp.minimum(g, nu[0] - 1), 0)),
                pl.BlockSpec(memory_space=pl.ANY),
                pl.BlockSpec((1, 1, 2 * d_ff), exp3),
                pl.BlockSpec(memory_space=pl.ANY),
                pl.BlockSpec((1, 1, D), exp3),
            ],
            out_specs=pl.BlockSpec((MOE_SUPER * ROW_TILES, LANES), lambda g, *_: (g, 0)),
            scratch_shapes=[
                pltpu.VMEM((2, D, 2 * d_ff), F32),
                pltpu.VMEM((2, d_ff, D), F32),
                pltpu.VMEM((D, 2 * d_ff), BF16),
                pltpu.VMEM((d_ff, D), BF16),
                pltpu.SMEM((1,), I32),
                pltpu.SemaphoreType.DMA((2, 2)),
            ],
        ),
        out_shape=jax.ShapeDtypeStruct(xs.shape, F32),
        compiler_params=_params("arbitrary"),
        name="moe_blocks",
    )(block_e, n_sub, first, nxt, n_used, xs, w_gu, b_gu, w_dn, b_dn)


def _combine_kernel(src_ref, len_ref, dst_ref, pos_ref, gate_ref, x1_ref, g_ref, b_ref, y_hbm, o_ref, ybuf, sem):
    i = pl.program_id(0)
    n_tiles = pl.num_programs(0)
    tt = x1_ref.shape[0]
    A = TOP_K * tt
    slot = i % 2

    def fetch(tile, s):
        for e in range(N_EXPERTS):
            seg = tile * N_EXPERTS + e
            src, dst = src_ref[seg], dst_ref[seg]
            _start_segment_copies(len_ref[seg], lambda off, p: pltpu.make_async_copy(
                y_hbm.at[_token_rows(dst + off, p), :], ybuf.at[s, _token_rows(src + off, p), :], sem.at[s]))

    @pl.when(i == 0)
    def _first_tile():
        fetch(0, 0)

    @pl.when(i + 1 < n_tiles)
    def _prefetch_next_tile():
        fetch(i + 1, 1 - slot)

    pltpu.make_async_copy(y_hbm.at[_token_rows(0, A), :], ybuf.at[slot], sem.at[slot]).wait()

    c = lax.broadcasted_iota(I32, (tt, A), 1)
    pos = pos_ref[...]
    gates = gate_ref[...]
    weights = jnp.zeros((tt, A), F32)
    for k in range(TOP_K):
        weights = jnp.where(c == pos[:, k:k + 1], gates[:, k:k + 1], weights)
    w_hi, w_lo = _split_bf16(weights)
    yl = _load_token_major(ybuf.at[slot], A).astype(BF16)
    y = _dot(w_hi, yl) + _dot(w_lo, yl)
    o_ref[...] = _layer_norm(DEEPNORM_ALPHA * x1_ref[...] + y, g_ref[...], b_ref[...])


def _combine(seg_src, seg_len, seg_dst, pos, gates, x1, g, b, y_buf, tt):
    T, D = x1.shape
    return pl.pallas_call(
        _combine_kernel,
        grid_spec=pltpu.PrefetchScalarGridSpec(
            num_scalar_prefetch=3,
            grid=(T // tt,),
            in_specs=[
                pl.BlockSpec((tt, TOP_K), lambda i, *_: (i, 0)),
                pl.BlockSpec((tt, TOP_K), lambda i, *_: (i, 0)),
                pl.BlockSpec((tt, D), lambda i, *_: (i, 0)),
                pl.BlockSpec((1, D), lambda i, *_: (0, 0)),
                pl.BlockSpec((1, D), lambda i, *_: (0, 0)),
                pl.BlockSpec(memory_space=pl.ANY),
            ],
            out_specs=pl.BlockSpec((tt, D), lambda i, *_: (i, 0)),
            scratch_shapes=[pltpu.VMEM((2, TOP_K * tt * ROW_TILES, LANES), F32), pltpu.SemaphoreType.DMA((2,))],
        ),
        out_shape=jax.ShapeDtypeStruct((T, D), F32),
        compiler_params=_params("arbitrary"),
        name="combine",
    )(seg_src, seg_len, seg_dst, pos, gates, x1, g, b, y_buf)


def _rotary_lane_tables(positions):
    half = ROT_DIM // 2
    inv_freq = ROPE_THETA ** (-jnp.arange(0, ROT_DIM, 2, dtype=F32) / ROT_DIM)
    ang = positions.reshape(-1).astype(F32)[:, None] * inv_freq
    cos, sin = jnp.cos(ang), jnp.sin(ang)
    r = jnp.arange(LANES) % HEAD_DIM
    first = (r < half)[None, :]
    second = ((r >= half) & (r < ROT_DIM))[None, :]
    cos_l = jnp.take(cos, r % half, axis=1)
    sin_l = jnp.take(sin, r % half, axis=1)
    c = jnp.where(first | second, cos_l, 1.0)
    s1 = jnp.where(first, -sin_l, 0.0)
    s2 = jnp.where(second, sin_l, 0.0)
    return c, s1, s2


def _route_metadata(tile_cnt, tile_before, n_super):
    tile_cnt = tile_cnt.astype(I32)
    tile_before = tile_before.astype(I32)
    experts = jnp.arange(N_EXPERTS, dtype=I32)
    counts = tile_before[-1] + tile_cnt[-1]
    padded = ((counts + MOE_SUPER - 1) // MOE_SUPER) * MOE_SUPER
    pad_end = jnp.cumsum(padded)
    pad_start = pad_end - padded
    n_used = pad_end[-1] // MOE_SUPER
    g = jnp.arange(n_super, dtype=I32)
    used = g < n_used
    g_row = jnp.minimum(g, n_used - 1) * MOE_SUPER
    block_e = jnp.clip(jnp.sum((pad_end[None, :] <= g_row[:, None]).astype(I32), axis=1), 0, N_EXPERTS - 1)
    of_block = lambda table: jnp.sum(jnp.where(block_e[:, None] == experts[None, :], table[None, :], 0), axis=1)
    rows_in = jnp.clip(of_block(counts) - (g_row - of_block(pad_start)), 0, MOE_SUPER)
    n_sub = jnp.where(used, (rows_in + MOE_BLOCK - 1) // MOE_BLOCK, 0)
    first = (used & (g_row == of_block(pad_start))).astype(I32)
    later_nonempty = (experts[None, :] > experts[:, None]) & (counts[None, :] > 0)
    nxt_e = jnp.min(jnp.where(later_nonempty, experts[None, :], N_EXPERTS), axis=1)
    nxt = of_block(jnp.where(nxt_e == N_EXPERTS, -1, nxt_e))
    tail = n_used + experts
    zrow = jnp.concatenate([jnp.where(padded > 0, pad_end - MOE_SUPER, -1),
                            jnp.where(tail < n_super, tail * MOE_SUPER, -1)]).astype(I32)
    seg_src = (jnp.cumsum(tile_cnt, axis=1) - tile_cnt).reshape(-1)
    seg_dst = (pad_start[None, :] + tile_before).reshape(-1)
    moe_scalars = (block_e, n_sub.astype(I32), first, nxt.astype(I32), n_used.reshape(1).astype(I32))
    return moe_scalars, zrow, seg_src, tile_cnt.reshape(-1), seg_dst


def kernel(x, positions, w_in, attn_gain, conv_w, conv_b, conv_ln_g, conv_ln_b, w_o, ln1_g, ln1_b,
           w_router, b_router, w_gu, b_gu, w_dn, b_dn, ln2_g, ln2_b):
    B, S, D = x.shape
    T = B * S
    depth = w_in.shape[0]
    tm = 512
    tt = 256
    n_super = T * TOP_K // MOE_SUPER + N_EXPERTS
    n_rows = n_super * MOE_SUPER

    rot_c, rot_s1, rot_s2 = _rotary_lane_tables(positions)
    row = lambda a: a.reshape(1, -1)

    x2d = x.reshape(T, D)
    for l in range(depth):
        qkv, yc = _in_proj_conv(x2d.reshape(B, S, D), w_in, l, rot_c, rot_s1, rot_s2, conv_w[l],
                                row(conv_b[l]), row(conv_ln_g[l]), row(conv_ln_b[l]), tm)
        ya = _moba_attention(qkv).reshape(T, ATTN_WIDTH)
        yc = yc.reshape(T, CONV_WIDTH)
        wr_hi, wr_lo = _split_bf16(w_router[l].T)
        x1, top_idx_t, gates_t = _mix_router(
            ya, yc, x2d, row(attn_gain[l]), w_o, l, row(ln1_g[l]), row(ln1_b[l]),
            wr_hi, wr_lo, b_router[l].reshape(N_EXPERTS, 1), tm)
        pos_t, tile_cnt, tile_before = _tile_sort_positions(top_idx_t, tt)
        moe_scalars, zrow, seg_src, seg_len, seg_dst = _route_metadata(
            tile_cnt[:, :, 0], tile_before[:, :, 0], n_super)
        xs = _dispatch(zrow, seg_src, seg_len, seg_dst, pos_t, x1, n_rows, tt)
        y_buf = _moe_blocks(*moe_scalars, xs, w_gu, b_gu[l][:, None, :], w_dn, b_dn[l][:, None, :], l)
        x2d = _combine(seg_src, seg_len, seg_dst, pos_t.T, gates_t.T, x1, row(ln2_g[l]), row(ln2_b[l]),
                       y_buf, tt)
    return x2d.reshape(B, S, D)
```

```python
import functools

import jax
import jax.numpy as jnp
from jax import lax
from jax.experimental import pallas as pl
from jax.experimental.pallas import tpu as pltpu

F32 = jnp.float32
BF16 = jnp.bfloat16
I32 = jnp.int32

HEAD_DIM = 64
ATTN_HEADS = 8
ATTN_WIDTH = ATTN_HEADS * HEAD_DIM
CONV_WIDTH = 512
ROT_DIM = HEAD_DIM // 4
ROPE_THETA = 500000.0
MOBA_BLOCK = 256
MOBA_TOPK = 3
CONV_KERNEL = 31
N_EXPERTS = 32
TOP_K = 4
MOE_BLOCK = 256
MOE_SUB_BLOCKS = 2
MOE_SUPER = MOE_BLOCK * MOE_SUB_BLOCKS
SWIGLU_LIMIT = 7.0
SWIGLU_ALPHA = 1.702
DEEPNORM_ALPHA = (2.0 * 4) ** 0.25
LN_EPS = 1e-5

LANES = 128
SUBLANES = 8
CONV_ROWS = 64
CONV_HALO = 32
MASK_PENALTY = -1e30
LOG2_E = 1.4426950408889634
MOBA_LOOKAHEAD = 1
VMEM_LIMIT = 48 * 1024 * 1024

_NT = (((1,), (1,)), ((), ()))


def _params(*sem):
    return pltpu.CompilerParams(dimension_semantics=sem, vmem_limit_bytes=VMEM_LIMIT)


def _dot(a, b):
    return jnp.dot(a, b, preferred_element_type=F32)


def _dot_nt(a, b):
    return lax.dot_general(a, b, _NT, preferred_element_type=F32)


def _split_bf16(a):
    hi = a.astype(BF16)
    lo = (a - hi.astype(F32)).astype(BF16)
    return hi, lo


def _layer_norm(z, g, b):
    mu = jnp.mean(z, axis=-1, keepdims=True)
    zc = z - mu
    var = jnp.mean(zc * zc, axis=-1, keepdims=True)
    return zc * lax.rsqrt(var + LN_EPS) * g + b


def _moba_kernel(q_ref, k_ref, v_ref, o_ref, kaug_ref, vaug_ref, s_ref):
    S = k_ref.shape[1]
    nb = S // MOBA_BLOCK
    BQ = MOBA_BLOCK
    scale = HEAD_DIM ** -0.5 * LOG2_E

    kf = k_ref[0].astype(F32)
    vf = v_ref[0].astype(F32)
    kmean = jnp.sum(kf.reshape(nb, MOBA_BLOCK, LANES), axis=1) / float(MOBA_BLOCK)
    lane_s = lax.broadcasted_iota(I32, (S, LANES), 1)
    blk = lax.broadcasted_iota(I32, (S, LANES), 0) // MOBA_BLOCK
    lane8 = lax.broadcasted_iota(I32, (nb, LANES), 1)
    km = []
    for h in range(2):
        own_s = (lane_s >= h * HEAD_DIM) & (lane_s < (h + 1) * HEAD_DIM)
        spare = (1 - h) * HEAD_DIM
        ind = jnp.where(lane_s == spare + blk, 1.0, 0.0)
        kaug_ref[h] = jnp.where(own_s, kf, ind).astype(BF16)
        vaug_ref[h] = jnp.where(own_s, vf, 1.0).astype(BF16)
        own8 = (lane8 >= h * HEAD_DIM) & (lane8 < (h + 1) * HEAD_DIM)
        km.append(_split_bf16(jnp.where(own8, kmean, 0.0)))

    lane = lax.broadcasted_iota(I32, (BQ, LANES), 1)
    jidx = lax.broadcasted_iota(I32, (nb, BQ), 0)
    causal = lax.broadcasted_iota(I32, (BQ, BQ), 1) <= lax.broadcasted_iota(I32, (BQ, BQ), 0)

    def scores(qb, h):
        q16 = q_ref[0, qb * BQ:(qb + 1) * BQ, :]
        own = (lane >= h * HEAD_DIM) & (lane < (h + 1) * HEAD_DIM)
        spare = (1 - h) * HEAD_DIM
        qs = jnp.where(own, q16.astype(F32) * scale, 0.0)
        if qb > MOBA_TOPK:
            gate = _dot_nt(km[h][0], q16) + _dot_nt(km[h][1], q16)
            gm = jnp.where(jidx < qb, gate, -jnp.inf)
            rank = jnp.zeros((nb, BQ), F32)
            for j2 in range(qb):
                other = gm[j2:j2 + 1, :]
                beats = (other > gm) | ((other == gm) & (j2 < jidx))
                rank = rank + jnp.where(beats, 1.0, 0.0)
            keep = (rank < float(MOBA_TOPK)) | (jidx >= qb)
            pen_t = jnp.where(keep, 0.0, MASK_PENALTY)
            pieces = [pen_t, jnp.zeros((LANES - spare - nb, BQ), F32)]
            if spare > 0:
                pieces.insert(0, jnp.zeros((spare, BQ), F32))
            q_aug = (qs + jnp.concatenate(pieces, axis=0).T).astype(BF16)
        else:
            q_aug = qs.astype(BF16)
        slot0 = qb * (qb + 1) // 2
        mt = None
        for j in range(qb + 1):
            s = _dot_nt(q_aug, kaug_ref[h, j * BQ:(j + 1) * BQ, :])
            if j == qb:
                s = jnp.where(causal, s, -jnp.inf)
            s_ref[h, slot0 + j] = s
            t = jnp.maximum(s[:, :LANES], s[:, LANES:])
            mt = t if mt is None else jnp.maximum(mt, t)
            yield
        row_max[(qb, h)] = jnp.max(mt, axis=-1, keepdims=True)

    def weighted_values(qb, h):
        m = row_max.pop((qb, h))
        slot0 = qb * (qb + 1) // 2
        acc = None
        for j in range(qb + 1):
            p = jnp.exp2(s_ref[h, slot0 + j] - m)
            pv = _dot(p.astype(BF16), vaug_ref[h, j * BQ:(j + 1) * BQ, :])
            acc = pv if acc is None else acc + pv
            yield
        spare = (1 - h) * HEAD_DIM
        outs[h] = acc / acc[:, spare:spare + 1]

    def emit(phase):
        for _ in phase:
            pass

    units = [(qb, h) for qb in range(nb) for h in range(2)]
    row_max, outs = {}, {}
    for u in units[:MOBA_LOOKAHEAD]:
        emit(scores(*u))
    for i, (qb, h) in enumerate(units):
        if i + MOBA_LOOKAHEAD < len(units):
            emit(scores(*units[i + MOBA_LOOKAHEAD]))
        emit(weighted_values(qb, h))
        if h == 1:
            o_ref[0, qb * BQ:(qb + 1) * BQ, :] = jnp.where(lane < HEAD_DIM, outs[0], outs[1])


def _moba_attention(qkv3):
    B, S, _ = qkv3.shape
    n_pairs = ATTN_WIDTH // LANES
    nb = S // MOBA_BLOCK
    return pl.pallas_call(
        _moba_kernel,
        grid=(B, n_pairs),
        in_specs=[
            pl.BlockSpec((1, S, LANES), lambda b, p: (b, 0, p)),
            pl.BlockSpec((1, S, LANES), lambda b, p: (b, 0, n_pairs + p)),
            pl.BlockSpec((1, S, LANES), lambda b, p: (b, 0, 2 * n_pairs + p)),
        ],
        out_specs=pl.BlockSpec((1, S, LANES), lambda b, p: (b, 0, p)),
        out_shape=jax.ShapeDtypeStruct((B, S, ATTN_WIDTH), F32),
        scratch_shapes=[
            pltpu.VMEM((2, S, LANES), BF16),
            pltpu.VMEM((2, S, LANES), BF16),
            pltpu.VMEM((2, nb * (nb + 1) // 2, MOBA_BLOCK, MOBA_BLOCK), F32),
        ],
        compiler_params=_params("parallel", "parallel"),
        name="moba_attention",
    )(qkv3, qkv3, qkv3)


def _conv_kernel(x_ref, wq32_ref, wa32_ref, wg32_ref, c_ref, s1_ref, s2_ref, cw_ref, cb_ref, g_ref, b_ref,
                 qkv_ref, o_ref, ubuf, w_ref, ush, wq_ref):
    si = pl.program_id(1)
    tm = x_ref.shape[1]

    @pl.when((pl.program_id(0) == 0) & (si == 0))
    def _cast_weights():
        w_ref[:, :CONV_WIDTH] = wa32_ref[...].astype(BF16)
        w_ref[:, CONV_WIDTH:] = wg32_ref[...].astype(BF16)
        wq_ref[...] = wq32_ref[...].astype(BF16)

    @pl.when(si == 0)
    def _zero_history():
        ubuf[0:CONV_HALO, :] = jnp.zeros((CONV_HALO, CONV_WIDTH), F32)

    @pl.when(si > 0)
    def _carry_history():
        ubuf[0:CONV_HALO, :] = ubuf[tm:tm + CONV_HALO, :]

    xb = x_ref[0].astype(BF16)
    u2 = _dot(xb, w_ref[...])
    ubuf[CONV_HALO:CONV_HALO + tm, :] = u2[:, :CONV_WIDTH] * jax.nn.sigmoid(u2[:, CONV_WIDTH:])
    n_sh = ush.shape[1]
    for s in range(1, SUBLANES):
        ush[s - 1] = ubuf[s:s + n_sh, :]

    reps = ATTN_WIDTH // LANES
    c = jnp.concatenate([c_ref[...]] * reps, axis=1)
    s1 = jnp.concatenate([s1_ref[...]] * reps, axis=1)
    s2 = jnp.concatenate([s2_ref[...]] * reps, axis=1)
    half = ROT_DIM // 2

    def project(part):
        cols = slice(part * ATTN_WIDTH, (part + 1) * ATTN_WIDTH)
        h = _dot(xb, wq_ref[:, cols])
        if part < 2:
            h = h * c + pltpu.roll(h, ATTN_WIDTH - half, 1) * s1 + pltpu.roll(h, half, 1) * s2
        qkv_ref[0, :, cols] = h.astype(BF16)

    base = CONV_HALO - (CONV_KERNEL - 1)

    def conv_rows(rc):
        parts = []
        for cg in range(CONV_WIDTH // LANES):
            lanes = slice(cg * LANES, (cg + 1) * LANES)
            acc = jnp.broadcast_to(cb_ref[:, lanes], (CONV_ROWS, LANES))
            for j in range(CONV_KERNEL):
                s = (base + j) % SUBLANES
                r0 = base + j - s + rc * CONV_ROWS
                src = ubuf[r0:r0 + CONV_ROWS, lanes] if s == 0 else ush[s - 1, r0:r0 + CONV_ROWS, lanes]
                acc = acc + cw_ref[j:j + 1, lanes] * src
            parts.append(acc)
        y = _layer_norm(jnp.concatenate(parts, axis=1), g_ref[...], b_ref[...])
        o_ref[0, rc * CONV_ROWS:(rc + 1) * CONV_ROWS, :] = (y * jax.nn.sigmoid(y)).astype(BF16)

    n_chunks = tm // CONV_ROWS
    project_before = {(part * n_chunks) // 3: part for part in range(3)}
    for rc in range(n_chunks):
        if rc in project_before:
            project(project_before[rc])
        conv_rows(rc)


def _in_proj_conv(x3, w_in, layer, rot_c, rot_s1, rot_s2, cw, cb, g, b, tm):
    B, S, D = x3.shape
    N = 3 * ATTN_WIDTH
    tiles = S // tm
    first = N // CONV_WIDTH
    table = pl.BlockSpec((tm, LANES), lambda bb, i: (bb * tiles + i, 0))
    vec = pl.BlockSpec((1, CONV_WIDTH), lambda bb, i: (0, 0))
    return pl.pallas_call(
        _conv_kernel,
        grid=(B, tiles),
        in_specs=[
            pl.BlockSpec((1, tm, D), lambda bb, i: (bb, i, 0)),
            pl.BlockSpec((None, D, N), lambda bb, i: (layer, 0, 0)),
            pl.BlockSpec((None, D, CONV_WIDTH), lambda bb, i: (layer, 0, first)),
            pl.BlockSpec((None, D, CONV_WIDTH), lambda bb, i: (layer, 0, first + 1)),
            table, table, table,
            pl.BlockSpec((CONV_KERNEL, CONV_WIDTH), lambda bb, i: (0, 0)),
            vec, vec, vec,
        ],
        out_specs=[
            pl.BlockSpec((1, tm, N), lambda bb, i: (bb, i, 0)),
            pl.BlockSpec((1, tm, CONV_WIDTH), lambda bb, i: (bb, i, 0)),
        ],
        out_shape=[
            jax.ShapeDtypeStruct((B, S, N), BF16),
            jax.ShapeDtypeStruct((B, S, CONV_WIDTH), BF16),
        ],
        scratch_shapes=[
            pltpu.VMEM((CONV_HALO + tm, CONV_WIDTH), F32),
            pltpu.VMEM((D, 2 * CONV_WIDTH), BF16),
            pltpu.VMEM((SUBLANES - 1, CONV_HALO + tm - SUBLANES, CONV_WIDTH), F32),
            pltpu.VMEM((D, N), BF16),
        ],
        compiler_params=_params("arbitrary", "arbitrary"),
        name="in_proj_conv",
    )(x3, w_in, w_in, w_in, rot_c, rot_s1, rot_s2, cw, cb, g, b)


def _mix_kernel(ya_ref, yc_ref, x_ref, gain_ref, wo32_ref, g_ref, b_ref, wrh_ref, wrl_ref, br_ref, tri_ref, low_ref,
                x1_ref, gate_ref, pos_ref, cnt_ref, before_ref, wo_ref, carry):
    tm = x_ref.shape[0]

    @pl.when(pl.program_id(0) == 0)
    def _first_step():
        wo_ref[...] = wo32_ref[...].astype(BF16)
        carry[...] = jnp.zeros_like(carry)

    ya = ya_ref[...]
    ms = jnp.mean(ya * ya, axis=-1, keepdims=True)
    yan = ya * lax.rsqrt(ms + LN_EPS) * gain_ref[...]
    mix = _dot(yan.astype(BF16), wo_ref[0:ATTN_WIDTH, :]) + _dot(yc_ref[...], wo_ref[ATTN_WIDTH:, :])
    x1 = _layer_norm(DEEPNORM_ALPHA * x_ref[...] + mix, g_ref[...], b_ref[...])
    x1_ref[...] = x1
    xh, xl = _split_bf16(x1)
    wrh = wrh_ref[...]
    logit = _dot_nt(wrh, xh) + _dot_nt(wrl_ref[...], xh) + _dot_nt(wrh, xl) + br_ref[...]
    eidx = lax.broadcasted_iota(I32, (N_EXPERTS, tm), 0)
    vals, idxs = [], []
    for _ in range(TOP_K):
        m = jnp.max(logit, axis=0, keepdims=True)
        idx = jnp.min(jnp.where(logit == m, eidx, N_EXPERTS), axis=0, keepdims=True)
        vals.append(m)
        idxs.append(idx)
        logit = jnp.where(eidx == idx, -jnp.inf, logit)
    es = [jnp.exp(v - vals[0]) for v in vals]
    tot = es[0] + es[1] + es[2] + es[3]
    gate_ref[...] = jnp.concatenate([e / tot for e in es], axis=0)

    tt = tri_ref.shape[0]
    for tile in range(tm // tt):
        cols = slice(tile * tt, (tile + 1) * tt)
        eidx_t = lax.broadcasted_iota(I32, (N_EXPERTS, tt), 0)
        onehots = [eidx_t == idx[:, cols] for idx in idxs]
        sel = jnp.zeros((N_EXPERTS, tt), F32)
        for oh in onehots:
            sel = sel + jnp.where(oh, 1.0, 0.0)
        prefix = _dot(sel.astype(BF16), tri_ref[...])
        cnt = jnp.broadcast_to(jnp.sum(sel, axis=1, keepdims=True), (N_EXPERTS, LANES))
        start = _dot(low_ref[...], cnt.astype(BF16))
        local = prefix + start[:, 0:1]
        pos = [jnp.sum(jnp.where(oh, local, 0.0), axis=0, keepdims=True) for oh in onehots]
        pos_ref[:, cols] = jnp.concatenate(pos, axis=0).astype(I32)
        cnt_ref[tile] = cnt
        before_ref[tile] = carry[...]
        carry[...] = carry[...] + cnt


def _mix_router(ya, yc, x2d, gain, w_o, layer, g, b, wr_hi, wr_lo, b_r, tm, tt):
    T, D = x2d.shape
    assert tt <= 256 and tm % tt == 0, "per-tile expert counts must stay exactly representable in bf16"
    tri = (jnp.arange(tt)[:, None] < jnp.arange(tt)[None, :]).astype(BF16)
    low = (jnp.arange(N_EXPERTS)[None, :] < jnp.arange(N_EXPERTS)[:, None]).astype(BF16)
    per_tile = pl.BlockSpec((tm // tt, N_EXPERTS, LANES), lambda i: (i, 0, 0))
    full = lambda shape: pl.BlockSpec(shape, lambda i: (0, 0))
    return pl.pallas_call(
        _mix_kernel,
        grid=(T // tm,),
        in_specs=[
            pl.BlockSpec((tm, ATTN_WIDTH), lambda i: (i, 0)),
            pl.BlockSpec((tm, CONV_WIDTH), lambda i: (i, 0)),
            pl.BlockSpec((tm, D), lambda i: (i, 0)),
            full((1, ATTN_WIDTH)),
            pl.BlockSpec((None, D, D), lambda i: (layer, 0, 0)),
            full((1, D)),
            full((1, D)),
            full((N_EXPERTS, D)),
            full((N_EXPERTS, D)),
            full((N_EXPERTS, 1)),
            full((tt, tt)),
            full((N_EXPERTS, N_EXPERTS)),
        ],
        out_specs=[
            pl.BlockSpec((tm, D), lambda i: (i, 0)),
            pl.BlockSpec((TOP_K, tm), lambda i: (0, i)),
            pl.BlockSpec((TOP_K, tm), lambda i: (0, i)),
            per_tile,
            per_tile,
        ],
        out_shape=[
            jax.ShapeDtypeStruct((T, D), F32),
            jax.ShapeDtypeStruct((TOP_K, T), F32),
            jax.ShapeDtypeStruct((TOP_K, T), I32),
            jax.ShapeDtypeStruct((T // tt, N_EXPERTS, LANES), F32),
            jax.ShapeDtypeStruct((T // tt, N_EXPERTS, LANES), F32),
        ],
        scratch_shapes=[pltpu.VMEM((D, D), BF16), pltpu.VMEM((N_EXPERTS, LANES), F32)],
        compiler_params=_params("arbitrary"),
        name="mix_router",
    )(ya, yc, x2d, gain, w_o, g, b, wr_hi, wr_lo, b_r, tri, low)


SEGMENT_CHUNKS = (256, 128, 64, 32, 16, 8, 4, 2, 1)
SEGMENT_RARE = 64


def _start_segment_copies(n_rows, make_copy):
    def arms(chunks):
        for p in chunks:
            offset = n_rows & ~(2 * p - 1)

            @pl.when((n_rows & p) != 0)
            def _():
                make_copy(offset, p).start()

    arms([p for p in SEGMENT_CHUNKS if p < SEGMENT_RARE])

    @pl.when(n_rows >= SEGMENT_RARE)
    def _():
        arms([p for p in SEGMENT_CHUNKS if p >= SEGMENT_RARE])


ROW_TILES = 8


def _token_rows(start, n):
    return pl.ds(pl.multiple_of(start * ROW_TILES, ROW_TILES), n * ROW_TILES)


def _store_token_major(ref, val):
    n = val.shape[0]
    for c in range(ROW_TILES):
        ref[pl.ds(c, n, stride=ROW_TILES), :] = val[:, c * LANES:(c + 1) * LANES]


def _load_token_major(ref, n):
    return jnp.concatenate([ref[pl.ds(c, n, stride=ROW_TILES), :] for c in range(ROW_TILES)], axis=1)


def _dispatch_kernel(zrow_ref, src_ref, len_ref, dst_ref, pos_ref, x_ref, xs_ref, sorted_ref, zbuf, zsem, sem):
    i = pl.program_id(0)
    n_tiles = pl.num_programs(0)
    tt = x_ref.shape[0]
    A = TOP_K * tt
    slot = i % 2

    def zero_copy(e):
        r = pl.multiple_of(jnp.maximum(zrow_ref[e], 0), MOE_SUPER)
        return pltpu.make_async_copy(zbuf, xs_ref.at[_token_rows(r, MOE_SUPER), :], zsem)

    @pl.when(pl.program_id(0) == 0)
    def _zero_padding_rows():
        zbuf[...] = jnp.zeros_like(zbuf)
        for e in range(2 * N_EXPERTS):
            @pl.when(zrow_ref[e] >= 0)
            def _():
                zero_copy(e).start()
        for e in range(2 * N_EXPERTS):
            @pl.when(zrow_ref[e] >= 0)
            def _():
                zero_copy(e).wait()

    def drain(s):
        pltpu.make_async_copy(sorted_ref.at[s], xs_ref.at[_token_rows(0, A), :], sem.at[s]).wait()

    @pl.when(i >= 2)
    def _slot_is_free():
        drain(slot)

    r = lax.broadcasted_iota(I32, (A, tt), 0)
    pos = pos_ref[...]
    onehot = jnp.zeros((A, tt), F32)
    for k in range(TOP_K):
        onehot = jnp.where(r == pos[k:k + 1, :], 1.0, onehot)
    _store_token_major(sorted_ref.at[slot], _dot(onehot.astype(BF16), x_ref[...].astype(BF16)))

    for e in range(N_EXPERTS):
        seg = i * N_EXPERTS + e
        src, dst = src_ref[seg], dst_ref[seg]
        _start_segment_copies(len_ref[seg], lambda off, p: pltpu.make_async_copy(
            sorted_ref.at[slot, _token_rows(src + off, p), :], xs_ref.at[_token_rows(dst + off, p), :],
            sem.at[slot]))

    @pl.when(i == n_tiles - 1)
    def _finish():
        drain(slot)

        @pl.when(i >= 1)
        def _():
            drain(1 - slot)


def _dispatch(zrow, seg_src, seg_len, seg_dst, pos_t, x1, n_rows, tt):
    T, D = x1.shape
    return pl.pallas_call(
        _dispatch_kernel,
        grid_spec=pltpu.PrefetchScalarGridSpec(
            num_scalar_prefetch=4,
            grid=(T // tt,),
            in_specs=[
                pl.BlockSpec((TOP_K, tt), lambda i, *_: (0, i)),
                pl.BlockSpec((tt, D), lambda i, *_: (i, 0)),
            ],
            out_specs=pl.BlockSpec(memory_space=pl.ANY),
            scratch_shapes=[
                pltpu.VMEM((2, TOP_K * tt * ROW_TILES, LANES), F32),
                pltpu.VMEM((MOE_SUPER * ROW_TILES, LANES), F32),
                pltpu.SemaphoreType.DMA(()),
                pltpu.SemaphoreType.DMA((2,)),
            ],
        ),
        out_shape=jax.ShapeDtypeStruct((n_rows * ROW_TILES, LANES), F32),
        compiler_params=_params("arbitrary"),
        name="dispatch",
    )(zrow, seg_src, seg_len, seg_dst, pos_t, x1)


def _moe_kernel(be_ref, nsub_ref, first_ref, nxt_ref, nu_ref, xs_ref, wgu_hbm, bgu_ref, wdn_hbm, bdn_ref, y_ref,
                wgu32, wdn32, wgu_ref, wdn_ref, slot_ref, sem, *, layer):
    d_ff = wdn_ref.shape[0]
    g = pl.program_id(0)

    def weight_copies(e, slot):
        return (pltpu.make_async_copy(wgu_hbm.at[layer, e], wgu32.at[slot], sem.at[0, slot]),
                pltpu.make_async_copy(wdn_hbm.at[layer, e], wdn32.at[slot], sem.at[1, slot]))

    @pl.when(g == 0)
    def _fetch_first_expert():
        slot_ref[0] = 0
        for cp in weight_copies(be_ref[0], 0):
            cp.start()

    @pl.when(first_ref[g] == 1)
    def _switch_expert():
        slot = slot_ref[0]
        for cp in weight_copies(be_ref[g], slot):
            cp.wait()
        wgu_ref[...] = wgu32[slot].astype(BF16)
        wdn_ref[...] = wdn32[slot].astype(BF16)

        @pl.when(nxt_ref[g] >= 0)
        def _fetch_next_expert():
            for cp in weight_copies(nxt_ref[g], 1 - slot):
                cp.start()

        slot_ref[0] = 1 - slot

    def sub_rows(sb):
        return pl.ds(sb * MOE_BLOCK * ROW_TILES, MOE_BLOCK * ROW_TILES)

    def expert_ffn(sb):
        x = _load_token_major(xs_ref.at[sub_rows(sb), :], MOE_BLOCK).astype(BF16)
        h = _dot(x, wgu_ref[...]) + bgu_ref[0]
        x_glu = jnp.minimum(h[:, :d_ff], SWIGLU_LIMIT)
        x_lin = jnp.clip(h[:, d_ff:], -SWIGLU_LIMIT, SWIGLU_LIMIT)
        act = x_glu * jax.nn.sigmoid(SWIGLU_ALPHA * x_glu) * (x_lin + 1.0)
        _store_token_major(y_ref.at[sub_rows(sb), :], _dot(act.astype(BF16), wdn_ref[...]) + bdn_ref[0])

    for n_real in range(MOE_SUB_BLOCKS + 1):
        @pl.when(nsub_ref[g] == n_real)
        def _():
            for sb in range(n_real):
                expert_ffn(sb)
            for sb in range(n_real, MOE_SUB_BLOCKS):
                y_ref[sub_rows(sb), :] = jnp.zeros((MOE_BLOCK * ROW_TILES, LANES), F32)


def _moe_blocks(block_e, n_sub, first, nxt, n_used, xs, w_gu, b_gu, w_dn, b_dn, layer):
    D = w_gu.shape[2]
    assert D == ROW_TILES * LANES
    n_super = xs.shape[0] // (MOE_SUPER * ROW_TILES)
    d_ff = w_dn.shape[2]
    exp3 = lambda g, be, ns, fi, nx, nu: (be[g], 0, 0)
    return pl.pallas_call(
        functools.partial(_moe_kernel, layer=layer),
        grid_spec=pltpu.PrefetchScalarGridSpec(
            num_scalar_prefetch=5,
            grid=(n_super,),
            in_specs=[
                pl.BlockSpec((MOE_SUPER * ROW_TILES, LANES),
                             lambda g, be, ns, fi, nx, nu: (jnp.minimum(g, nu[0] - 1), 0)),
                pl.BlockSpec(memory_space=pl.ANY),
                pl.BlockSpec((1, 1, 2 * d_ff), exp3),
                pl.BlockSpec(memory_space=pl.ANY),
                pl.BlockSpec((1, 1, D), exp3),
            ],
            out_specs=pl.BlockSpec((MOE_SUPER * ROW_TILES, LANES), lambda g, *_: (g, 0)),
            scratch_shapes=[
                pltpu.VMEM((2, D, 2 * d_ff), F32),
                pltpu.VMEM((2, d_ff, D), F32),
                pltpu.VMEM((D, 2 * d_ff), BF16),
                pltpu.VMEM((d_ff, D), BF16),
                pltpu.SMEM((1,), I32),
                pltpu.SemaphoreType.DMA((2, 2)),
            ],
        ),
        out_shape=jax.ShapeDtypeStruct(xs.shape, F32),
        compiler_params=_params("arbitrary"),
        name="moe_blocks",
    )(block_e, n_sub, first, nxt, n_used, xs, w_gu, b_gu, w_dn, b_dn)


def _combine_kernel(src_ref, len_ref, dst_ref, pos_ref, gate_ref, x1_ref, g_ref, b_ref, y_hbm, o_ref, ybuf, sem):
    i = pl.program_id(0)
    n_tiles = pl.num_programs(0)
    tt = x1_ref.shape[0]
    A = TOP_K * tt
    slot = i % 2

    def fetch(tile, s):
        for e in range(N_EXPERTS):
            seg = tile * N_EXPERTS + e
            src, dst = src_ref[seg], dst_ref[seg]
            _start_segment_copies(len_ref[seg], lambda off, p: pltpu.make_async_copy(
                y_hbm.at[_token_rows(dst + off, p), :], ybuf.at[s, _token_rows(src + off, p), :], sem.at[s]))

    @pl.when(i == 0)
    def _first_tile():
        fetch(0, 0)

    @pl.when(i + 1 < n_tiles)
    def _prefetch_next_tile():
        fetch(i + 1, 1 - slot)

    pltpu.make_async_copy(y_hbm.at[_token_rows(0, A), :], ybuf.at[slot], sem.at[slot]).wait()

    c = lax.broadcasted_iota(I32, (tt, A), 1)
    pos = pos_ref[...]
    gates = gate_ref[...]
    weights = jnp.zeros((tt, A), F32)
    for k in range(TOP_K):
        weights = jnp.where(c == pos[:, k:k + 1], gates[:, k:k + 1], weights)
    w_hi, w_lo = _split_bf16(weights)
    yl = _load_token_major(ybuf.at[slot], A).astype(BF16)
    y = _dot(w_hi, yl) + _dot(w_lo, yl)
    o_ref[...] = _layer_norm(DEEPNORM_ALPHA * x1_ref[...] + y, g_ref[...], b_ref[...])


def _combine(seg_src, seg_len, seg_dst, pos, gates, x1, g, b, y_buf, tt):
    T, D = x1.shape
    return pl.pallas_call(
        _combine_kernel,
        grid_spec=pltpu.PrefetchScalarGridSpec(
            num_scalar_prefetch=3,
            grid=(T // tt,),
            in_specs=[
                pl.BlockSpec((tt, TOP_K), lambda i, *_: (i, 0)),
                pl.BlockSpec((tt, TOP_K), lambda i, *_: (i, 0)),
                pl.BlockSpec((tt, D), lambda i, *_: (i, 0)),
                pl.BlockSpec((1, D), lambda i, *_: (0, 0)),
                pl.BlockSpec((1, D), lambda i, *_: (0, 0)),
                pl.BlockSpec(memory_space=pl.ANY),
            ],
            out_specs=pl.BlockSpec((tt, D), lambda i, *_: (i, 0)),
            scratch_shapes=[pltpu.VMEM((2, TOP_K * tt * ROW_TILES, LANES), F32), pltpu.SemaphoreType.DMA((2,))],
        ),
        out_shape=jax.ShapeDtypeStruct((T, D), F32),
        compiler_params=_params("arbitrary"),
        name="combine",
    )(seg_src, seg_len, seg_dst, pos, gates, x1, g, b, y_buf)


def _rotary_lane_tables(positions):
    half = ROT_DIM // 2
    inv_freq = ROPE_THETA ** (-jnp.arange(0, ROT_DIM, 2, dtype=F32) / ROT_DIM)
    ang = positions.reshape(-1).astype(F32)[:, None] * inv_freq
    cos, sin = jnp.cos(ang), jnp.sin(ang)
    r = jnp.arange(LANES) % HEAD_DIM
    first = (r < half)[None, :]
    second = ((r >= half) & (r < ROT_DIM))[None, :]
    cos_l = jnp.take(cos, r % half, axis=1)
    sin_l = jnp.take(sin, r % half, axis=1)
    c = jnp.where(first | second, cos_l, 1.0)
    s1 = jnp.where(first, -sin_l, 0.0)
    s2 = jnp.where(second, sin_l, 0.0)
    return c, s1, s2


def _route_metadata(tile_cnt, tile_before, n_super):
    tile_cnt = tile_cnt.astype(I32)
    tile_before = tile_before.astype(I32)
    experts = jnp.arange(N_EXPERTS, dtype=I32)
    counts = tile_before[-1] + tile_cnt[-1]
    padded = ((counts + MOE_SUPER - 1) // MOE_SUPER) * MOE_SUPER
    pad_end = jnp.cumsum(padded)
    pad_start = pad_end - padded
    n_used = pad_end[-1] // MOE_SUPER
    g = jnp.arange(n_super, dtype=I32)
    used = g < n_used
    g_row = jnp.minimum(g, n_used - 1) * MOE_SUPER
    block_e = jnp.clip(jnp.sum((pad_end[None, :] <= g_row[:, None]).astype(I32), axis=1), 0, N_EXPERTS - 1)
    of_block = lambda table: jnp.sum(jnp.where(block_e[:, None] == experts[None, :], table[None, :], 0), axis=1)
    rows_in = jnp.clip(of_block(counts) - (g_row - of_block(pad_start)), 0, MOE_SUPER)
    n_sub = jnp.where(used, (rows_in + MOE_BLOCK - 1) // MOE_BLOCK, 0)
    first = (used & (g_row == of_block(pad_start))).astype(I32)
    later_nonempty = (experts[None, :] > experts[:, None]) & (counts[None, :] > 0)
    nxt_e = jnp.min(jnp.where(later_nonempty, experts[None, :], N_EXPERTS), axis=1)
    nxt = of_block(jnp.where(nxt_e == N_EXPERTS, -1, nxt_e))
    tail = n_used + experts
    zrow = jnp.concatenate([jnp.where(padded > 0, pad_end - MOE_SUPER, -1),
                            jnp.where(tail < n_super, tail * MOE_SUPER, -1)]).astype(I32)
    seg_src = (jnp.cumsum(tile_cnt, axis=1) - tile_cnt).reshape(-1)
    seg_dst = (pad_start[None, :] + tile_before).reshape(-1)
    moe_scalars = (block_e, n_sub.astype(I32), first, nxt.astype(I32), n_used.reshape(1).astype(I32))
    return moe_scalars, zrow, seg_src, tile_cnt.reshape(-1), seg_dst


def kernel(x, positions, w_in, attn_gain, conv_w, conv_b, conv_ln_g, conv_ln_b, w_o, ln1_g, ln1_b,
           w_router, b_router, w_gu, b_gu, w_dn, b_dn, ln2_g, ln2_b):
    B, S, D = x.shape
    T = B * S
    depth = w_in.shape[0]
    tm = 512
    tt = 256
    n_super = T * TOP_K // MOE_SUPER + N_EXPERTS
    n_rows = n_super * MOE_SUPER

    rot_c, rot_s1, rot_s2 = _rotary_lane_tables(positions)
    row = lambda a: a.reshape(1, -1)

    x2d = x.reshape(T, D)
    for l in range(depth):
        qkv, yc = _in_proj_conv(x2d.reshape(B, S, D), w_in, l, rot_c, rot_s1, rot_s2, conv_w[l],
                                row(conv_b[l]), row(conv_ln_g[l]), row(conv_ln_b[l]), tm)
        ya = _moba_attention(qkv).reshape(T, ATTN_WIDTH)
        yc = yc.reshape(T, CONV_WIDTH)
        wr_hi, wr_lo = _split_bf16(w_router[l].T)
        x1, gates_t, pos_t, tile_cnt, tile_before = _mix_router(
            ya, yc, x2d, row(attn_gain[l]), w_o, l, row(ln1_g[l]), row(ln1_b[l]),
            wr_hi, wr_lo, b_router[l].reshape(N_EXPERTS, 1), tm, tt)
        moe_scalars, zrow, seg_src, seg_len, seg_dst = _route_metadata(
            tile_cnt[:, :, 0], tile_before[:, :, 0], n_super)
        xs = _dispatch(zrow, seg_src, seg_len, seg_dst, pos_t, x1, n_rows, tt)
        y_buf = _moe_blocks(*moe_scalars, xs, w_gu, b_gu[l][:, None, :], w_dn, b_dn[l][:, None, :], l)
        x2d = _combine(seg_src, seg_len, seg_dst, pos_t.T, gates_t.T, x1, row(ln2_g[l]), row(ln2_b[l]),
                       y_buf, tt)
    return x2d.reshape(B, S, D)
```
